```python
import jax, jax.numpy as jnp
from jax import lax
import numpy as np

D_MODEL = 1024
BATCH = 16
SEQ = 2048
DEPTH = 4
DEC_BATCH = 128
DEC_SEQ = 1
PAST_LEN = 8192
PAGE_SIZE = 128

N_A_LAYERS = DEPTH // 2
N_B_LAYERS = DEPTH - N_A_LAYERS
EXPAND = 2
D_INNER = EXPAND * D_MODEL
A_HEAD = 64
A_HEADS = D_INNER // A_HEAD
DECAY_LORA = 64
ICLR_LORA = 64
VRES_LORA = 32
GN_EPS = A_HEAD * 1e-5
B_HEADS = 16
QK_NOPE = 128
QK_ROPE = 64
V_HEAD = D_INNER // B_HEADS
KV_LORA = 256
Q_LORA = 384
ROPE_THETA = 10000.0
Q_BLOCK = 128
NORM_EPS = 1e-6
ATTN_SCALE = (QK_NOPE + QK_ROPE) ** -0.5

kernel_name = 'yoco_rwkv7_mla_decoder_step'

F32 = jnp.float32


def rmsnorm(x, w):
    x32 = x.astype(F32)
    y = x32 * lax.rsqrt(jnp.mean(x32 * x32, axis=-1, keepdims=True) + NORM_EPS)
    return (y * w.astype(F32)).astype(x.dtype)


def rope(x, pos):
    half = x.shape[-1] // 2
    inv_freq = ROPE_THETA ** (-jnp.arange(half, dtype=F32) / half)
    ang = pos.astype(F32)[:, None] * inv_freq[None, :]
    shape = (1, pos.shape[0]) + (1,) * (x.ndim - 3) + (half,)
    cos = jnp.cos(ang).reshape(shape)
    sin = jnp.sin(ang).reshape(shape)
    x32 = x.astype(F32)
    x1, x2 = x32[..., :half], x32[..., half:]
    return jnp.concatenate([x1 * cos - x2 * sin, x2 * cos + x1 * sin], axis=-1).astype(x.dtype)


def _wkv_step(S, inp):
    r, w, k, v, a, b = inp
    sa = jnp.einsum('bhvk,bhk->bhv', S, a)
    S = S * w[:, :, None, :] + sa[..., None] * b[:, :, None, :] + v[..., None] * k[:, :, None, :]
    return S, jnp.einsum('bhvk,bhk->bhv', S, r)


def rwkv7_mix(h, h_prev, S0, v_first, vres, mix, w_in, w0, w1, w2, a0, a1, a2,
              k_k, k_a, r_k, lnx_w, lnx_b, w_out):
    B, T, _ = h.shape
    shifted = jnp.concatenate([h_prev[:, None, :].astype(h.dtype), h[:, :-1, :]], axis=1)
    dx = shifted - h
    mixed = h[None] + dx[None] * mix[:4, None, None, :]
    r, k, v, z = jnp.einsum('pbtd,pde->pbte', mixed, w_in)
    xw = h + dx * mix[4]
    xa = h + dx * mix[5]
    w_log = -jax.nn.softplus(-(w0 + jnp.tanh(xw @ w1) @ w2)) - 0.5
    decay = jnp.exp(-jnp.exp(w_log.astype(F32)))
    a = jax.nn.sigmoid(a0 + (xa @ a1) @ a2)
    if vres is not None:
        v0, v1, v2 = vres
        v = v + (v_first - v) * jax.nn.sigmoid(v0 + (mixed[2] @ v1) @ v2)
    heads = lambda t: t.reshape(B, T, A_HEADS, A_HEAD)
    kk = heads(k * k_k).astype(F32)
    kk = kk / jnp.maximum(jnp.sqrt(jnp.sum(kk * kk, axis=-1, keepdims=True)), 1e-12)
    k = k * (1 + (a - 1) * k_a)
    rh, kh, vh, ah = heads(r), heads(k), heads(v), heads(a)
    seq = lambda t: jnp.moveaxis(t.astype(F32), 1, 0)
    xs = (seq(rh), seq(heads(decay)), seq(kh), seq(vh), seq(-kk), seq(kk * ah.astype(F32)))
    S_fin, y = lax.scan(_wkv_step, S0.astype(F32), xs)
    y = jnp.moveaxis(y, 0, 1)
    mu = jnp.mean(y, axis=-1, keepdims=True)
    var = jnp.mean(jnp.square(y - mu), axis=-1, keepdims=True)
    yn = ((y - mu) * lax.rsqrt(var + GN_EPS)).reshape(B, T, D_INNER)
    yn = (yn * lnx_w.astype(F32) + lnx_b.astype(F32)).astype(h.dtype)
    bonus = (jnp.sum(rh * kh * r_k, axis=-1, keepdims=True) * vh).reshape(B, T, D_INNER)
    out = ((yn + bonus) * jax.nn.silu(z)) @ w_out
    return out, v, S_fin.astype(S0.dtype), h[:, -1, :]


def mla_shared_kv(x, pos, kv_in_norm_w, w_dkv, kv_norm_w):
    kv = rmsnorm(x, kv_in_norm_w) @ w_dkv
    ckv = rmsnorm(kv[..., :KV_LORA], kv_norm_w)
    kpe = rope(kv[..., KV_LORA:], pos)
    return ckv, kpe


def mla_query(h, pos, w_in, q_norm_w, w_uq):
    B, T, _ = h.shape
    proj = h @ w_in
    cq = rmsnorm(proj[..., :Q_LORA], q_norm_w)
    z = proj[..., Q_LORA:]
    q = (cq @ w_uq).reshape(B, T, B_HEADS, QK_NOPE + QK_ROPE)
    return q[..., :QK_NOPE], rope(q[..., QK_NOPE:], pos), z


def mla_prompt_attn(q_nope, q_pe, k_nope, kpe, v):
    B, S = q_nope.shape[:2]
    nblk = S // Q_BLOCK
    qn = jnp.moveaxis(q_nope.reshape(B, nblk, Q_BLOCK, B_HEADS, QK_NOPE), 1, 0)
    qp = jnp.moveaxis(q_pe.reshape(B, nblk, Q_BLOCK, B_HEADS, QK_ROPE), 1, 0)
    kpos = jnp.arange(S)

    def block(args):
        i, qn_b, qp_b = args
        s = (jnp.einsum('bqhd,bshd->bhqs', qn_b, k_nope)
             + jnp.einsum('bqhr,bsr->bhqs', qp_b, kpe)).astype(F32) * ATTN_SCALE
        qpos = i * Q_BLOCK + jnp.arange(Q_BLOCK)
        s = jnp.where(kpos[None, :] <= qpos[:, None], s, -jnp.inf)
        p = jax.nn.softmax(s, axis=-1).astype(v.dtype)
        return jnp.einsum('bhqs,bshv->bqhv', p, v)

    o = lax.map(block, (jnp.arange(nblk), qn, qp))
    return jnp.moveaxis(o, 0, 1).reshape(B, S, B_HEADS * V_HEAD)


def mla_sample_attn(q_nope, q_pe, ckv_past, kpe_past, ckv_new, kpe_new, w_uk, w_uv):
    B, T = q_nope.shape[:2]
    P = ckv_past.shape[1]
    q_lat = jnp.einsum('bthn,chn->bthc', q_nope, w_uk)
    s_past = (jnp.einsum('bthc,bsc->bhts', q_lat, ckv_past)
              + jnp.einsum('bthr,bsr->bhts', q_pe, kpe_past)).astype(F32) * ATTN_SCALE
    s_new = (jnp.einsum('bthc,bsc->bhts', q_lat, ckv_new)
             + jnp.einsum('bthr,bsr->bhts', q_pe, kpe_new)).astype(F32) * ATTN_SCALE
    causal = jnp.arange(T)[None, :] <= jnp.arange(T)[:, None]
    s_new = jnp.where(causal, s_new, -jnp.inf)
    p = jax.nn.softmax(jnp.concatenate([s_past, s_new], axis=-1), axis=-1).astype(ckv_past.dtype)
    o_lat = (jnp.einsum('bhts,bsc->bthc', p[..., :P], ckv_past)
             + jnp.einsum('bhts,bsc->bthc', p[..., P:], ckv_new))
    o = jnp.einsum('bthc,chv->bthv', o_lat, w_uv)
    return o.reshape(B, T, B_HEADS * V_HEAD)


def setup_inputs(seed: int = 0) -> dict:
    key = jax.random.key(seed)
    ks = iter(jax.random.split(key, 48))
    nrm = lambda shape, scale: jax.random.normal(next(ks), shape, F32) * scale
    n_pages = PAST_LEN // PAGE_SIZE
    n_used = DEC_BATCH * n_pages
    n_pool = n_used + (n_used + 3) // 4
    page_table = jax.random.permutation(next(ks), n_pool)[:n_used].astype(jnp.int32).reshape(DEC_BATCH, n_pages)
    NA, NB = N_A_LAYERS, N_B_LAYERS
    return {
        'x_prompt': nrm((BATCH, SEQ, D_MODEL), 1.0),
        'x_sample': nrm((DEC_BATCH, DEC_SEQ, D_MODEL), 1.0),
        'state_wkv': nrm((NA, DEC_BATCH, A_HEADS, A_HEAD, A_HEAD), 0.5),
        'state_shift': nrm((NA, DEC_BATCH, D_MODEL), 1.0),
        'cache_ckv': nrm((n_pool, PAGE_SIZE, KV_LORA), 1.0),
        'cache_kpe': nrm((n_pool, PAGE_SIZE, QK_ROPE), 1.0),
        'page_table': page_table,
        'a_norm_w': 1.0 + nrm((NA, D_MODEL), 0.02),
        'a_mix': jax.random.uniform(next(ks), (NA, 6, D_MODEL), F32),
        'a_w_in': nrm((NA, 4, D_MODEL, D_INNER), D_MODEL ** -0.5),
        'a_w0': jax.random.uniform(next(ks), (NA, D_INNER), F32, -6.0, -1.0),
        'a_w1': nrm((NA, D_MODEL, DECAY_LORA), D_MODEL ** -0.5),
        'a_w2': nrm((NA, DECAY_LORA, D_INNER), 0.1 * DECAY_LORA ** -0.5),
        'a_a0': nrm((NA, D_INNER), 0.1),
        'a_a1': nrm((NA, D_MODEL, ICLR_LORA), D_MODEL ** -0.5),
        'a_a2': nrm((NA, ICLR_LORA, D_INNER), 0.5 * ICLR_LORA ** -0.5),
        'a_v0': nrm((NA - 1, D_INNER), 0.1),
        'a_v1': nrm((NA - 1, D_MODEL, VRES_LORA), D_MODEL ** -0.5),
        'a_v2': nrm((NA - 1, VRES_LORA, D_INNER), 0.5 * VRES_LORA ** -0.5),
        'a_k_k': 0.85 + nrm((NA, D_INNER), 0.02),
        'a_k_a': 1.0 + nrm((NA, D_INNER), 0.02),
        'a_r_k': nrm((NA, A_HEADS, A_HEAD), 0.1),
        'a_lnx_w': 1.0 + nrm((NA, D_INNER), 0.02),
        'a_lnx_b': nrm((NA, D_INNER), 0.01),
        'a_w_out': nrm((NA, D_INNER, D_MODEL), D_INNER ** -0.5),
        'kv_in_norm_w': 1.0 + nrm((D_MODEL,), 0.02),
        'w_dkv': nrm((D_MODEL, KV_LORA + QK_ROPE), D_MODEL ** -0.5),
        'kv_norm_w': 1.0 + nrm((KV_LORA,), 0.02),
        'w_uk': nrm((KV_LORA, B_HEADS, QK_NOPE), KV_LORA ** -0.5),
        'w_uv': nrm((KV_LORA, B_HEADS, V_HEAD), KV_LORA ** -0.5),
        'b_norm_w': 1.0 + nrm((NB, D_MODEL), 0.02),
        'b_w_in': nrm((NB, D_MODEL, Q_LORA + D_INNER), D_MODEL ** -0.5),
        'b_q_norm_w': 1.0 + nrm((NB, Q_LORA), 0.02),
        'b_w_uq': nrm((NB, Q_LORA, B_HEADS * (QK_NOPE + QK_ROPE)), Q_LORA ** -0.5),
        'b_w_out': nrm((NB, D_INNER, D_MODEL), D_INNER ** -0.5),
        'final_norm_w': 1.0 + nrm((D_MODEL,), 0.02),
    }


def reference(x_prompt, x_sample, state_wkv, state_shift, cache_ckv, cache_kpe, page_table,
              a_norm_w, a_mix, a_w_in, a_w0, a_w1, a_w2, a_a0, a_a1, a_a2, a_v0, a_v1, a_v2,
              a_k_k, a_k_a, a_r_k, a_lnx_w, a_lnx_b, a_w_out,
              kv_in_norm_w, w_dkv, kv_norm_w, w_uk, w_uv,
              b_norm_w, b_w_in, b_q_norm_w, b_w_uq, b_w_out, final_norm_w):
    Bp, Sp, _ = x_prompt.shape
    Bs, Ts, _ = x_sample.shape
    past = page_table.shape[1] * cache_ckv.shape[1]
    pos_p = jnp.arange(Sp, dtype=jnp.int32)
    pos_s = past + jnp.arange(Ts, dtype=jnp.int32)
    xp, xs = x_prompt, x_sample
    shift0 = jnp.zeros((Bp, D_MODEL), x_prompt.dtype)
    wkv0 = jnp.zeros((Bp, A_HEADS, A_HEAD, A_HEAD), state_wkv.dtype)
    vf_p = vf_s = None
    wkv_p, sh_p, wkv_s, sh_s = [], [], [], []
    for l in range(N_A_LAYERS):
        vres = None if l == 0 else (a_v0[l - 1], a_v1[l - 1], a_v2[l - 1])
        prm = (a_mix[l], a_w_in[l], a_w0[l], a_w1[l], a_w2[l], a_a0[l], a_a1[l], a_a2[l],
               a_k_k[l], a_k_a[l], a_r_k[l], a_lnx_w[l], a_lnx_b[l], a_w_out[l])
        o, v, S, last = rwkv7_mix(rmsnorm(xp, a_norm_w[l]), shift0, wkv0, vf_p, vres, *prm)
        xp = xp + o
        wkv_p.append(S)
        sh_p.append(last)
        if l == 0:
            vf_p = v
        o, v, S, last = rwkv7_mix(rmsnorm(xs, a_norm_w[l]), state_shift[l], state_wkv[l], vf_s, vres, *prm)
        xs = xs + o
        wkv_s.append(S)
        sh_s.append(last)
        if l == 0:
            vf_s = v
    ckv_p, kpe_p = mla_shared_kv(xp, pos_p, kv_in_norm_w, w_dkv, kv_norm_w)
    ckv_s, kpe_s = mla_shared_kv(xs, pos_s, kv_in_norm_w, w_dkv, kv_norm_w)
    k_nope_p = jnp.einsum('bsc,chn->bshn', ckv_p, w_uk)
    v_p = jnp.einsum('bsc,chv->bshv', ckv_p, w_uv)
    ckv_past = cache_ckv[page_table].reshape(Bs, past, KV_LORA)
    kpe_past = cache_kpe[page_table].reshape(Bs, past, QK_ROPE)
    for l in range(N_B_LAYERS):
        qn, qr, z = mla_query(rmsnorm(xp, b_norm_w[l]), pos_p, b_w_in[l], b_q_norm_w[l], b_w_uq[l])
        o = mla_prompt_attn(qn, qr, k_nope_p, kpe_p, v_p)
        xp = xp + (o * jax.nn.silu(z)) @ b_w_out[l]
        qn, qr, z = mla_query(rmsnorm(xs, b_norm_w[l]), pos_s, b_w_in[l], b_q_norm_w[l], b_w_uq[l])
        o = mla_sample_attn(qn, qr, ckv_past, kpe_past, ckv_s, kpe_s, w_uk, w_uv)
        xs = xs + (o * jax.nn.silu(z)) @ b_w_out[l]
    y_prompt = rmsnorm(xp, final_norm_w)
    y_sample = rmsnorm(xs, final_norm_w)
    return (y_prompt, y_sample, jnp.stack(wkv_p), jnp.stack(sh_p), ckv_p, kpe_p,
            jnp.stack(wkv_s), jnp.stack(sh_s), ckv_s, kpe_s)
```

```python
import functools
import math

import jax
import jax.numpy as jnp
from jax import lax
from jax.experimental import pallas as pl
from jax.experimental.pallas import tpu as pltpu

F32 = jnp.float32
BF16 = jnp.bfloat16

NORM_EPS = 1e-6
ROPE_THETA = 10000.0
A_HEAD = 64
GN_EPS = A_HEAD * 1e-5
WKV_CHUNK = 64
WKV_GROUP_HEADS = 8
V7X_VMEM_LIMIT = 48 * 1024 * 1024

_NT = (((1,), (1,)), ((), ()))
_TN = (((0,), (0,)), ((), ()))


def _params(*sem):
    return pltpu.CompilerParams(dimension_semantics=sem, vmem_limit_bytes=V7X_VMEM_LIMIT)


def _dot(a, b):
    return jnp.dot(a.astype(BF16), b.astype(BF16), preferred_element_type=F32)


def _split2(x):
    hi = x.astype(BF16)
    lo = (x - hi.astype(F32)).astype(BF16)
    return hi, lo


def _split3(x):
    hi = x.astype(BF16)
    r = x - hi.astype(F32)
    mid = r.astype(BF16)
    lo = (r - mid.astype(F32)).astype(BF16)
    return hi, mid, lo


def _sigmoid(x):
    return 1.0 / (1.0 + jnp.exp(-x))


def _rms(x, w):
    return x * lax.rsqrt(jnp.mean(x * x, axis=-1, keepdims=True) + NORM_EPS) * w


def _premix_seq_body(seq_tiles, x_ref, prev_ref, nw_ref, mix_ref, mixed_ref, hlast_ref, carry_ref):
    i = pl.program_id(0)
    h = _rms(x_ref[...], nw_ref[...])
    tm = h.shape[0]
    prev_row = jnp.where(i % seq_tiles == 0, prev_ref[0], carry_ref[...])
    row = lax.broadcasted_iota(jnp.int32, h.shape, 0)
    shifted = jnp.where(row == 0, prev_row, pltpu.roll(h, 1, axis=0))
    last = h[tm - 1:tm, :]
    carry_ref[...] = last
    hlast_ref[0] = last
    dx = shifted - h
    for p in range(6):
        mixed_ref[p] = (h + dx * mix_ref[p:p + 1, :]).astype(BF16)


def _premix_tok_body(x_ref, prev_ref, nw_ref, mix_ref, mixed_ref, hlast_ref):
    h = _rms(x_ref[...], nw_ref[...])
    hlast_ref[...] = h
    dx = prev_ref[...] - h
    for p in range(6):
        mixed_ref[p] = (h + dx * mix_ref[p:p + 1, :]).astype(BF16)


def _premix(x, prev, norm_w, mix, seq_len):
    M, D = x.shape
    B = M // seq_len
    nw = norm_w.reshape(1, D)
    out_mixed = jax.ShapeDtypeStruct((6, M, D), BF16)
    if seq_len == 1:
        tm = min(M, 256)
        return pl.pallas_call(
            _premix_tok_body,
            grid=(M // tm,),
            in_specs=[pl.BlockSpec((tm, D), lambda i: (i, 0)),
                      pl.BlockSpec((tm, D), lambda i: (i, 0)),
                      pl.BlockSpec((1, D), lambda i: (0, 0)),
                      pl.BlockSpec((6, D), lambda i: (0, 0))],
            out_specs=[pl.BlockSpec((6, tm, D), lambda i: (0, i, 0)),
                       pl.BlockSpec((tm, D), lambda i: (i, 0))],
            out_shape=[out_mixed, jax.ShapeDtypeStruct((B, D), F32)],
            compiler_params=_params("arbitrary"),
            name="rwkv_premix_tok",
        )(x, prev, nw, mix)
    tm = min(seq_len, 256)
    seq_tiles = seq_len // tm
    mixed, hlast = pl.pallas_call(
        functools.partial(_premix_seq_body, seq_tiles),
        grid=(M // tm,),
        in_specs=[pl.BlockSpec((tm, D), lambda i: (i, 0)),
                  pl.BlockSpec((1, 1, D), lambda i: (i // seq_tiles, 0, 0)),
                  pl.BlockSpec((1, D), lambda i: (0, 0)),
                  pl.BlockSpec((6, D), lambda i: (0, 0))],
        out_specs=[pl.BlockSpec((6, tm, D), lambda i: (0, i, 0)),
                   pl.BlockSpec((1, 1, D), lambda i: (i // seq_tiles, 0, 0))],
        out_shape=[out_mixed, jax.ShapeDtypeStruct((B, 1, D), F32)],
        scratch_shapes=[pltpu.VMEM((1, D), F32)],
        compiler_params=_params("arbitrary"),
        name="rwkv_premix_seq",
    )(x, prev.reshape(B, 1, D), nw, mix)
    return mixed, hlast.reshape(B, D)


def _bmm_body(x_ref, w_ref, o_ref):
    o_ref[0] = jnp.dot(x_ref[0], w_ref[0], preferred_element_type=F32).astype(o_ref.dtype)


def _bmm(x, w, out_dtype, n_batch, tm=512, tn=2048):
    _, M, K = x.shape
    P, _, N = w.shape
    assert P == n_batch
    tm, tn = min(tm, M), min(tn, N)
    return pl.pallas_call(
        _bmm_body,
        grid=(P, N // tn, M // tm),
        in_specs=[pl.BlockSpec((1, tm, K), lambda p, n, m: (p, m, 0)),
                  pl.BlockSpec((1, K, tn), lambda p, n, m: (p, 0, n))],
        out_specs=pl.BlockSpec((1, tm, tn), lambda p, n, m: (p, m, n)),
        out_shape=jax.ShapeDtypeStruct((P, M, N), out_dtype),
        compiler_params=_params("arbitrary", "arbitrary", "arbitrary"),
        name="rwkv_in_proj",
    )(x, w)


def _lora_body(has_vres, *refs):
    if has_vres:
        (xw_ref, xa_ref, xv_ref, k_ref, v_ref, vf_ref,
         w1_ref, w2_ref, w0_ref, a1_ref, a2_ref, a0_ref, v1_ref, v2_ref, v0_ref,
         kk_ref, ka_ref, e_ref, et_ref,
         lw_ref, am_ref, bm_ref, kp_ref, vp_ref) = refs
    else:
        (xw_ref, xa_ref, k_ref,
         w1_ref, w2_ref, w0_ref, a1_ref, a2_ref, a0_ref,
         kk_ref, ka_ref, e_ref, et_ref,
         lw_ref, am_ref, bm_ref, kp_ref) = refs
    xd = w0_ref[...] + _dot(jnp.tanh(_dot(xw_ref[0], w1_ref[...])), w2_ref[...])
    lw_ref[...] = -math.exp(-0.5) * _sigmoid(xd)
    alr = _sigmoid(a0_ref[...] + _dot(_dot(xa_ref[0], a1_ref[...]), a2_ref[...]))
    k = k_ref[0]
    kk = k * kk_ref[...]
    sq_hi, sq_lo = _split2(kk * kk)
    e = e_ref[...]
    ss = jnp.dot(sq_hi, e, preferred_element_type=F32) + jnp.dot(sq_lo, e, preferred_element_type=F32)
    inv = 1.0 / jnp.maximum(jnp.sqrt(ss), 1e-12)
    et = et_ref[...]
    i_hi, i_mid, i_lo = _split3(inv)
    inv_b = (jnp.dot(i_hi, et, preferred_element_type=F32)
             + jnp.dot(i_mid, et, preferred_element_type=F32)
             + jnp.dot(i_lo, et, preferred_element_type=F32))
    kk = kk * inv_b
    am_ref[...] = -kk
    bm_ref[...] = kk * alr
    kp_ref[...] = k * (1.0 + (alr - 1.0) * ka_ref[...])
    if has_vres:
        v = v_ref[0]
        vg = _sigmoid(v0_ref[...] + _dot(_dot(xv_ref[0], v1_ref[...]), v2_ref[...]))
        vp_ref[...] = v + (vf_ref[...] - v) * vg


def _lora(mixed, rkvz, v_first, lw_w, lora_a, lora_v, k_k, k_a, e, et):
    _, M, D = mixed.shape
    DI = rkvz.shape[2]
    tm = min(M, 128)
    has_vres = lora_v is not None
    row = lambda i: (i, 0)
    full = lambda i: (0, 0)
    mixed_spec = lambda p: pl.BlockSpec((1, tm, D), lambda i: (p, i, 0))
    rkvz_spec = lambda p: pl.BlockSpec((1, tm, DI), lambda i: (p, i, 0))
    wspec = lambda a: pl.BlockSpec(a.shape, full)
    ins = [mixed, mixed]
    specs = [mixed_spec(4), mixed_spec(5)]
    if has_vres:
        ins += [mixed]
        specs += [mixed_spec(2)]
    ins += [rkvz]
    specs += [rkvz_spec(1)]
    if has_vres:
        ins += [rkvz, v_first]
        specs += [rkvz_spec(2), pl.BlockSpec((tm, DI), row)]
    weights = list(lw_w) + list(lora_a) + (list(lora_v) if has_vres else []) + [k_k, k_a, e, et]
    ins += weights
    specs += [wspec(a) for a in weights]
    n_out = 5 if has_vres else 4
    return pl.pallas_call(
        functools.partial(_lora_body, has_vres),
        grid=(M // tm,),
        in_specs=specs,
        out_specs=[pl.BlockSpec((tm, DI), row)] * n_out,
        out_shape=[jax.ShapeDtypeStruct((M, DI), F32)] * n_out,
        compiler_params=_params("arbitrary"),
        name="rwkv_lora",
    )(*ins)


def _unit_lower_inverse(low):
    c = low.shape[0]
    eye = (lax.broadcasted_iota(jnp.int32, (c, c), 0) == lax.broadcasted_iota(jnp.int32, (c, c), 1)).astype(F32)
    inv = eye + low
    x = low
    span = 2
    while span < c:
        x = _dot(x, x)
        inv = inv + _dot(inv, x)
        span *= 2
    return inv


def _wkv_body(r_ref, k_ref, v_ref, lw_ref, a_ref, b_ref, s0_ref, y_ref, sfin_ref, s_scr):
    j = pl.program_id(2)
    C = WKV_CHUNK
    N = A_HEAD
    n_heads = s_scr.shape[0]
    n_chunks = r_ref.shape[0] // C

    @pl.when(j == 0)
    def _():
        s_scr[...] = s0_ref[0]

    row = lax.broadcasted_iota(jnp.int32, (C, C), 0)
    col = lax.broadcasted_iota(jnp.int32, (C, C), 1)
    incl = row >= col
    strict = row > col
    tri = incl.astype(BF16)
    row2 = lax.broadcasted_iota(jnp.int32, (C, 2 * C), 0)
    col2 = lax.broadcasted_iota(jnp.int32, (C, 2 * C), 1)
    incl2 = row2 >= jnp.where(col2 >= C, col2 - C, col2)

    def chunk(c, carry):
        rows = pl.ds(pl.multiple_of(c * C, C), C)
        lw = lw_ref[rows, :]
        l_hi, l_mid, l_lo = _split3(lw)
        cs = (jnp.dot(tri, l_hi, preferred_element_type=F32)
              + jnp.dot(tri, l_mid, preferred_element_type=F32)
              + jnp.dot(tri, l_lo, preferred_element_type=F32))
        cs_last = cs[C - 1:C, :]
        a, b, k, r, v = a_ref[rows, :], b_ref[rows, :], k_ref[rows, :], r_ref[rows, :], v_ref[rows, :]
        e_neg = jnp.exp(-cs)
        e_rem = jnp.exp(cs_last - cs)
        at = (a * jnp.exp(cs - lw)).astype(BF16)
        rt = (r * jnp.exp(cs)).astype(BF16)
        bt = (b * e_neg).astype(BF16)
        kt = (k * e_neg).astype(BF16)
        bw = (b * e_rem).astype(BF16)
        kw = (k * e_rem).astype(BF16)
        vb = v.astype(BF16)
        w_end = jnp.exp(cs_last)
        ys = []
        for h in range(n_heads):
            sl = slice(h * N, (h + 1) * N)
            ar = jnp.concatenate([at[:, sl], rt[:, sl]], axis=0)
            bk = jnp.concatenate([bt[:, sl], kt[:, sl]], axis=0)
            p = lax.dot_general(ar, bk, _NT, preferred_element_type=F32)
            low = jnp.where(strict, p[:C, :C], 0.0)
            m_ak = jnp.where(strict, p[:C, C:], 0.0)
            m_r = jnp.where(incl2, p[C:, :], 0.0)
            inv = _unit_lower_inverse(low)
            s = s_scr[h]
            ay0 = lax.dot_general(ar, s.astype(BF16), _NT, preferred_element_type=F32)
            u = _dot(inv, ay0[:C] + _dot(m_ak, vb[:, sl]))
            uv = jnp.concatenate([u.astype(BF16), vb[:, sl]], axis=0)
            ys.append(ay0[C:] + _dot(m_r, uv))
            bkw = jnp.concatenate([bw[:, sl], kw[:, sl]], axis=0)
            s_scr[h] = s * w_end[:, sl] + lax.dot_general(uv, bkw, _TN, preferred_element_type=F32)
        y_ref[rows, :] = jnp.concatenate(ys, axis=1)
        return carry

    lax.fori_loop(0, n_chunks, chunk, 0)

    @pl.when(j == pl.num_programs(2) - 1)
    def _():
        sfin_ref[0] = s_scr[...]


def _wkv_seq(r, k, v, lw, a, b, s0, seq_len):
    M, DI = r.shape
    B = M // seq_len
    H = DI // A_HEAD
    gh = min(WKV_GROUP_HEADS, H)
    gl = gh * A_HEAD
    tb = min(seq_len, 256)
    nt = seq_len // tb
    blk = pl.BlockSpec((tb, gl), lambda bi, g, j: (bi * nt + j, g))
    sblk = pl.BlockSpec((1, gh, A_HEAD, A_HEAD), lambda bi, g, j: (bi, g, 0, 0))
    return pl.pallas_call(
        _wkv_body,
        grid=(B, H // gh, nt),
        in_specs=[blk] * 6 + [sblk],
        out_specs=[blk, sblk],
        out_shape=[jax.ShapeDtypeStruct((M, DI), F32), jax.ShapeDtypeStruct(s0.shape, F32)],
        scratch_shapes=[pltpu.VMEM((gh, A_HEAD, A_HEAD), F32)],
        compiler_params=_params("arbitrary", "arbitrary", "arbitrary"),
        name="rwkv_wkv_chunked",
    )(r, k, v, lw, a, b, s0)


def _wkv_tok_body(s_ref, r_ref, lw_ref, k_ref, vcol_ref, a_ref, b_ref, snew_ref, y_ref):
    s = s_ref[...]
    bc = lambda ref: ref[...][:, :, None, :]
    sa = jnp.sum(s * bc(a_ref), axis=-1, keepdims=True)
    s_new = s * jnp.exp(bc(lw_ref)) + sa * bc(b_ref) + vcol_ref[...] * bc(k_ref)
    snew_ref[...] = s_new
    y_ref[...] = jnp.sum(s_new * bc(r_ref), axis=-1, keepdims=True)


def _wkv_tok(s0, r, lw, k, v, a, b):
    B, H, N, _ = s0.shape
    bb = min(B, 2)
    heads = lambda t: t.reshape(B, H, N)
    vcol = jnp.broadcast_to(heads(v)[..., None], (B, H, N, N))
    sspec = pl.BlockSpec((bb, H, N, N), lambda i: (i, 0, 0, 0))
    vspec = pl.BlockSpec((bb, H, N), lambda i: (i, 0, 0))
    s_new, y = pl.pallas_call(
        _wkv_tok_body,
        grid=(B // bb,),
        in_specs=[sspec, vspec, vspec, vspec, sspec, vspec, vspec],
        out_specs=[sspec, pl.BlockSpec((bb, H, N, 1), lambda i: (i, 0, 0, 0))],
        out_shape=[jax.ShapeDtypeStruct(s0.shape, F32), jax.ShapeDtypeStruct((B, H, N, 1), F32)],
        compiler_params=_params("arbitrary"),
        name="rwkv_wkv_step",
    )(s0, heads(r), heads(lw), heads(k), vcol, heads(a), heads(b))
    return y.reshape(B, H * N), s_new


def _rwkv_out_body(y_ref, r_ref, kp_ref, vp_ref, z_ref, x_ref, lw_ref, lb_ref, rk_ref,
                   e_ref, et_ref, wo_ref, o_ref):
    n = float(A_HEAD)
    e = e_ref[...]
    et = et_ref[...]
    seg = lambda t: jnp.dot(t.astype(BF16), e, preferred_element_type=F32)

    def spread(t):
        hi, lo = _split2(t)
        return jnp.dot(hi, et, preferred_element_type=F32) + jnp.dot(lo, et, preferred_element_type=F32)

    y = y_ref[...]
    yc = y - spread(seg(y) / n)
    yc = yc - spread(seg(yc) / n)
    var = seg(yc * yc) / n
    yn = yc * spread(lax.rsqrt(var + GN_EPS)) * lw_ref[...] + lb_ref[...]
    vp = vp_ref[0] if len(vp_ref.shape) == 3 else vp_ref[...]
    bonus = spread(seg(r_ref[0] * kp_ref[...] * rk_ref[...])) * vp
    z = z_ref[0]
    g = (yn + bonus) * (z * _sigmoid(z))
    o_ref[...] = x_ref[...] + jnp.dot(g.astype(BF16), wo_ref[...], preferred_element_type=F32)


def _rwkv_out(y, rkvz, kp, vp, x, lnx_w, lnx_b, r_k, e, et, w_out):
    M, DI = y.shape
    D = x.shape[1]
    tm = min(M, 256)
    row = lambda i: (i, 0)
    full = lambda i: (0, 0)
    rkvz_spec = lambda p: pl.BlockSpec((1, tm, DI), lambda i: (p, i, 0))
    if vp is None:
        vp_in, vp_spec = rkvz, rkvz_spec(2)
    else:
        vp_in, vp_spec = vp, pl.BlockSpec((tm, DI), row)
    vec = pl.BlockSpec((1, DI), full)
    return pl.pallas_call(
        _rwkv_out_body,
        grid=(M // tm,),
        in_specs=[pl.BlockSpec((tm, DI), row), rkvz_spec(0), pl.BlockSpec((tm, DI), row), vp_spec,
                  rkvz_spec(3), pl.BlockSpec((tm, D), row), vec, vec, vec,
                  pl.BlockSpec(e.shape, full), pl.BlockSpec(et.shape, full),
                  pl.BlockSpec(w_out.shape, full)],
        out_specs=pl.BlockSpec((tm, D), row),
        out_shape=jax.ShapeDtypeStruct((M, D), F32),
        compiler_params=_params("arbitrary"),
        name="rwkv_out_proj",
    )(y, rkvz, kp, vp_in, rkvz, x, lnx_w, lnx_b, r_k, e, et, w_out)


def _mla_kv_body(c_lat, r_dim, x_ref, nw_ref, w_ref, kvn_ref, cos_ref, sin_ref, ckv_ref, kpe_ref):
    kv = _dot(_rms(x_ref[...], nw_ref[...]), w_ref[...])
    ckv_ref[...] = _rms(kv[:, :c_lat], kvn_ref[...])
    pe = kv[:, c_lat:c_lat + r_dim]
    pe_rot = kv[:, c_lat + 128:c_lat + 128 + r_dim]
    kpe_ref[...] = pe * cos_ref[...] + pe_rot * sin_ref[...]


def _mla_kv(x, norm_w, w_ext, kv_norm_w, cos, sin, c_lat, r_dim, seq_len):
    M, D = x.shape
    tm = min(seq_len if seq_len > 1 else M, 256)
    pos_tiles = max(seq_len // tm, 1)
    row = lambda i: (i, 0)
    full = lambda i: (0, 0)
    pos = lambda i: (i % pos_tiles, 0)
    return pl.pallas_call(
        functools.partial(_mla_kv_body, c_lat, r_dim),
        grid=(M // tm,),
        in_specs=[pl.BlockSpec((tm, D), row), pl.BlockSpec((1, D), full), pl.BlockSpec(w_ext.shape, full),
                  pl.BlockSpec((1, c_lat), full), pl.BlockSpec((tm, r_dim), pos), pl.BlockSpec((tm, r_dim), pos)],
        out_specs=[pl.BlockSpec((tm, c_lat), row), pl.BlockSpec((tm, r_dim), row)],
        out_shape=[jax.ShapeDtypeStruct((M, c_lat), F32), jax.ShapeDtypeStruct((M, r_dim), F32)],
        compiler_params=_params("arbitrary"),
        name="mla_shared_kv",
    )(x, norm_w, w_ext, kv_norm_w, cos, sin)


def _mla_expand_body(n_heads, d_nope, d_v, ckv_ref, kpe_ref, w_ref, k_ref, v_ref):
    kv = _dot(ckv_ref[...], w_ref[...]).astype(BF16)
    pe = kpe_ref[...].astype(BF16)
    v0 = n_heads * d_nope
    for h in range(n_heads):
        k_ref[h, :, :d_nope] = kv[:, h * d_nope:(h + 1) * d_nope]
        k_ref[h, :, d_nope:] = pe
        v_ref[h] = kv[:, v0 + h * d_v:v0 + (h + 1) * d_v]


def _mla_expand(ckv, kpe, w_ukv, n_heads, d_nope, d_v):
    M, c_lat = ckv.shape
    r_dim = kpe.shape[1]
    tm = min(M, 256)
    row = lambda i: (i, 0)
    return pl.pallas_call(
        functools.partial(_mla_expand_body, n_heads, d_nope, d_v),
        grid=(M // tm,),
        in_specs=[pl.BlockSpec((tm, c_lat), row), pl.BlockSpec((tm, r_dim), row),
                  pl.BlockSpec(w_ukv.shape, lambda i: (0, 0))],
        out_specs=[pl.BlockSpec((n_heads, tm, d_nope + r_dim), lambda i: (0, i, 0)),
                   pl.BlockSpec((n_heads, tm, d_v), lambda i: (0, i, 0))],
        out_shape=[jax.ShapeDtypeStruct((n_heads, M, d_nope + r_dim), BF16),
                   jax.ShapeDtypeStruct((n_heads, M, d_v), BF16)],
        compiler_params=_params("arbitrary"),
        name="mla_expand_kv",
    )(ckv, kpe, w_ukv)


def _mla_q_body(n_heads, d_nope, r_dim, q_lat, scale,
                x_ref, nw_ref, win_ref, qn_ref, wuq_ref, cos_ref, sin_ref, q_ref, z_ref):
    proj = _dot(_rms(x_ref[...], nw_ref[...]), win_ref[...])
    z_ref[...] = proj[:, q_lat:]
    q = _dot(_rms(proj[:, :q_lat], qn_ref[...]), wuq_ref[...]) * scale
    n0 = n_heads * d_nope
    n1 = n0 + n_heads * r_dim
    pe = q[:, n0:n1] * cos_ref[...] + q[:, n1:] * sin_ref[...]
    for h in range(n_heads):
        q_ref[h, :, :d_nope] = q[:, h * d_nope:(h + 1) * d_nope].astype(BF16)
        q_ref[h, :, d_nope:] = pe[:, h * r_dim:(h + 1) * r_dim].astype(BF16)


def _mla_q(x, norm_w, w_in, q_norm_w, w_uq_ext, cos_q, sin_q, n_heads, d_nope, r_dim, scale, seq_len):
    M, D = x.shape
    q_lat = q_norm_w.shape[1]
    d_gate = w_in.shape[1] - q_lat
    tm = min(seq_len if seq_len > 1 else M, 256)
    pos_tiles = max(seq_len // tm, 1)
    row = lambda i: (i, 0)
    full = lambda i: (0, 0)
    pos = lambda i: (i % pos_tiles, 0)
    return pl.pallas_call(
        functools.partial(_mla_q_body, n_heads, d_nope, r_dim, q_lat, scale),
        grid=(M // tm,),
        in_specs=[pl.BlockSpec((tm, D), row), pl.BlockSpec((1, D), full), pl.BlockSpec(w_in.shape, full),
                  pl.BlockSpec((1, q_lat), full), pl.BlockSpec(w_uq_ext.shape, full),
                  pl.BlockSpec((tm, n_heads * r_dim), pos), pl.BlockSpec((tm, n_heads * r_dim), pos)],
        out_specs=[pl.BlockSpec((n_heads, tm, d_nope + r_dim), lambda i: (0, i, 0)),
                   pl.BlockSpec((tm, d_gate), row)],
        out_shape=[jax.ShapeDtypeStruct((n_heads, M, d_nope + r_dim), BF16),
                   jax.ShapeDtypeStruct((M, d_gate), F32)],
        compiler_params=_params("arbitrary"),
        name="mla_query",
    )(x, norm_w, w_in, q_norm_w, w_uq_ext, cos_q, sin_q)


def _flash_body(q_ref, k_ref, v_ref, o_ref, m_scr, l_scr, acc_scr):
    qi = pl.program_id(2)
    ki = pl.program_id(3)
    tq = q_ref.shape[1]
    tk = k_ref.shape[1]

    @pl.when(ki == 0)
    def _():
        m_scr[...] = jnp.full(m_scr.shape, -jnp.inf, F32)
        l_scr[...] = jnp.zeros(l_scr.shape, F32)
        acc_scr[...] = jnp.zeros(acc_scr.shape, F32)

    def update(masked):
        s = lax.dot_general(q_ref[0], k_ref[0], _NT, preferred_element_type=F32)
        if masked:
            qpos = qi * tq + lax.broadcasted_iota(jnp.int32, (tq, tk), 0)
            kpos = ki * tk + lax.broadcasted_iota(jnp.int32, (tq, tk), 1)
            s = jnp.where(kpos <= qpos, s, -jnp.inf)
        m_old = m_scr[...]
        m_new = jnp.maximum(m_old, jnp.max(s, axis=-1, keepdims=True))
        alpha = jnp.exp(m_old - m_new)
        p = jnp.exp(s - m_new[:, :1])
        l_scr[...] = alpha * l_scr[...] + jnp.sum(p, axis=-1, keepdims=True)
        acc_scr[...] = alpha[:, :1] * acc_scr[...] + jnp.dot(p.astype(BF16), v_ref[0], preferred_element_type=F32)
        m_scr[...] = m_new

    @pl.when(ki < qi)
    def _():
        update(False)

    @pl.when(ki == qi)
    def _():
        update(True)
        o_ref[...] = acc_scr[...] / l_scr[...][:, :1]


def _flash(q, k, v, seq_len):
    H, M, dk = q.shape
    dv = v.shape[2]
    B = M // seq_len
    t = min(seq_len, 512)
    nb = seq_len // t
    return pl.pallas_call(
        _flash_body,
        grid=(B, H, nb, nb),
        in_specs=[pl.BlockSpec((1, t, dk), lambda b, h, qi, ki: (h, b * nb + qi, 0)),
                  pl.BlockSpec((1, t, dk), lambda b, h, qi, ki: (h, b * nb + jnp.minimum(ki, qi), 0)),
                  pl.BlockSpec((1, t, dv), lambda b, h, qi, ki: (h, b * nb + jnp.minimum(ki, qi), 0))],
        out_specs=pl.BlockSpec((t, dv), lambda b, h, qi, ki: (b * nb + qi, h)),
        out_shape=jax.ShapeDtypeStruct((M, H * dv), F32),
        scratch_shapes=[pltpu.VMEM((t, 128), F32), pltpu.VMEM((t, 128), F32), pltpu.VMEM((t, dv), F32)],
        compiler_params=_params("arbitrary", "arbitrary", "arbitrary", "arbitrary"),
        name="mla_prompt_attention",
    )(q, k, v)


def _mla_out_body(with_norm, o_ref, z_ref, x_ref, wo_ref, fw_ref, xo_ref, *maybe_y):
    z = z_ref[...]
    g = o_ref[...] * (z * _sigmoid(z))
    xn = x_ref[...] + jnp.dot(g.astype(BF16), wo_ref[...], preferred_element_type=F32)
    xo_ref[...] = xn
    if with_norm:
        maybe_y[0][...] = _rms(xn, fw_ref[...])


def _mla_out(o, z, x, w_out, final_w, with_norm):
    M, DI = o.shape
    D = x.shape[1]
    tm = min(M, 256)
    row = lambda i: (i, 0)
    full = lambda i: (0, 0)
    n_out = 2 if with_norm else 1
    outs = pl.pallas_call(
        functools.partial(_mla_out_body, with_norm),
        grid=(M // tm,),
        in_specs=[pl.BlockSpec((tm, DI), row), pl.BlockSpec((tm, DI), row), pl.BlockSpec((tm, D), row),
                  pl.BlockSpec(w_out.shape, full), pl.BlockSpec((1, D), full)],
        out_specs=[pl.BlockSpec((tm, D), row)] * n_out,
        out_shape=[jax.ShapeDtypeStruct((M, D), F32)] * n_out,
        compiler_params=_params("arbitrary"),
        name="mla_out_proj",
    )(o, z, x, w_out, final_w)
    return outs if with_norm else (outs[0], None)


def _headwise_mm_body(x_ref, w_ref, o_ref):
    o_ref[0] = jnp.dot(x_ref[0].astype(BF16), w_ref[0], preferred_element_type=F32).astype(o_ref.dtype)


def _headwise_mm(x, w, out_dtype, name):
    H, B, K = x.shape
    N = w.shape[2]
    return pl.pallas_call(
        _headwise_mm_body,
        grid=(H,),
        in_specs=[pl.BlockSpec((1, B, K), lambda h: (h, 0, 0)), pl.BlockSpec((1, K, N), lambda h: (h, 0, 0))],
        out_specs=pl.BlockSpec((1, B, N), lambda h: (h, 0, 0)),
        out_shape=jax.ShapeDtypeStruct((H, B, N), out_dtype),
        compiler_params=_params("arbitrary"),
        name=name,
    )(x, w)


DECODE_PAGES_PER_STEP = 8


def _decode_body(n_pg, pt_ref, ql_ref, qp_ref, cn_ref, pn_ref, *refs):
    ckv_refs = refs[:n_pg]
    kpe_refs = refs[n_pg:2 * n_pg]
    o_ref, m_scr, l_scr, acc_scr = refs[2 * n_pg:]
    j = pl.program_id(1)
    ql = ql_ref[0]
    qp = qp_ref[0]

    @pl.when(j == 0)
    def _():
        cn = cn_ref[0]
        s_new = (jnp.sum(ql.astype(F32) * cn, axis=-1, keepdims=True)
                 + jnp.sum(qp.astype(F32) * pn_ref[0], axis=-1, keepdims=True))
        m_scr[...] = jnp.broadcast_to(s_new, m_scr.shape)
        l_scr[...] = jnp.ones(l_scr.shape, F32)
        acc_scr[...] = jnp.broadcast_to(cn, acc_scr.shape)

    ckv = jnp.concatenate([r[0] for r in ckv_refs], axis=0).astype(BF16)
    kpe = jnp.concatenate([r[0] for r in kpe_refs], axis=0).astype(BF16)
    s = (lax.dot_general(ql, ckv, _NT, preferred_element_type=F32)
         + lax.dot_general(qp, kpe, _NT, preferred_element_type=F32))
    m_old = m_scr[...]
    m_new = jnp.maximum(m_old, jnp.max(s, axis=-1, keepdims=True))
    alpha = jnp.exp(m_old - m_new)
    p = jnp.exp(s - m_new[:, :1])
    l_new = alpha * l_scr[...] + jnp.sum(p, axis=-1, keepdims=True)
    acc = alpha[:, :1] * acc_scr[...] + jnp.dot(p.astype(BF16), ckv, preferred_element_type=F32)
    m_scr[...] = m_new
    l_scr[...] = l_new
    acc_scr[...] = acc

    @pl.when(j == pl.num_programs(1) - 1)
    def _():
        o_ref[0] = acc / l_new[:, :1]


def _decode_attn(q_lat, q_pe, ckv_new, kpe_new, cache_ckv, cache_kpe, page_table):
    B, H, C = q_lat.shape
    R = q_pe.shape[2]
    page = cache_ckv.shape[1]
    n_pages = page_table.shape[1]
    n_pg = min(DECODE_PAGES_PER_STEP, n_pages)
    ckv_spec = lambda i: pl.BlockSpec((1, page, C), lambda b, j, pt: (pt[b, j * n_pg + i], 0, 0))
    kpe_spec = lambda i: pl.BlockSpec((1, page, R), lambda b, j, pt: (pt[b, j * n_pg + i], 0, 0))
    per_b = lambda b, j, pt: (b, 0, 0)
    grid_spec = pltpu.PrefetchScalarGridSpec(
        num_scalar_prefetch=1,
        grid=(B, n_pages // n_pg),
        in_specs=[pl.BlockSpec((1, H, C), per_b), pl.BlockSpec((1, H, R), per_b),
                  pl.BlockSpec((1, 1, C), per_b), pl.BlockSpec((1, 1, R), per_b)]
                 + [ckv_spec(i) for i in range(n_pg)] + [kpe_spec(i) for i in range(n_pg)],
        out_specs=pl.BlockSpec((1, H, C), per_b),
        scratch_shapes=[pltpu.VMEM((H, 128), F32), pltpu.VMEM((H, 128), F32), pltpu.VMEM((H, C), F32)],
    )
    return pl.pallas_call(
        functools.partial(_decode_body, n_pg),
        grid_spec=grid_spec,
        out_shape=jax.ShapeDtypeStruct((B, H, C), F32),
        compiler_params=_params("arbitrary", "arbitrary"),
        name="mla_sample_attention",
    )(page_table, q_lat, q_pe, ckv_new, kpe_new, *([cache_ckv] * n_pg), *([cache_kpe] * n_pg))


def _rot_half_cols(w, r_dim):
    lead = w.shape[:-1]
    wb = w.reshape(lead + (-1, 2, r_dim // 2))
    return jnp.stack([-wb[..., 1, :], wb[..., 0, :]], axis=-2).reshape(w.shape)


def _rope_tables(pos, r_dim):
    half = r_dim // 2
    inv_freq = ROPE_THETA ** (-jnp.arange(half, dtype=F32) / half)
    ang = pos.astype(F32)[:, None] * inv_freq[None, :]
    cos, sin = jnp.cos(ang), jnp.sin(ang)
    return jnp.concatenate([cos, cos], axis=1), jnp.concatenate([sin, sin], axis=1)


def kernel(x_prompt, x_sample, state_wkv, state_shift, cache_ckv, cache_kpe, page_table, a_norm_w, a_mix, a_w_in, a_w0, a_w1, a_w2, a_a0, a_a1, a_a2, a_v0, a_v1, a_v2, a_k_k, a_k_a, a_r_k, a_lnx_w, a_lnx_b, a_w_out, kv_in_norm_w, w_dkv, kv_norm_w, w_uk, w_uv, b_norm_w, b_w_in, b_q_norm_w, b_w_uq, b_w_out, final_norm_w):
    Bp, Sp, D = x_prompt.shape
    Bs, Ts, _ = x_sample.shape
    assert Ts == 1
    n_a = a_norm_w.shape[0]
    n_b = b_norm_w.shape[0]
    DI = a_w_in.shape[3]
    H = DI // A_HEAD
    c_lat, n_bh, d_nope = w_uk.shape
    d_v = w_uv.shape[2]
    r_dim = cache_kpe.shape[2]
    past = page_table.shape[1] * cache_ckv.shape[1]
    scale = float(d_nope + r_dim) ** -0.5
    bf = lambda t: t.astype(BF16)
    vec = lambda t: t.reshape(1, -1)

    head_of_lane = jnp.arange(DI, dtype=jnp.int32) // A_HEAD
    e_seg = (head_of_lane[:, None] == jnp.arange(H, dtype=jnp.int32)[None, :]).astype(BF16)
    et_seg = e_seg.T

    xp = x_prompt.reshape(Bp * Sp, D)
    xs = x_sample.reshape(Bs, D)
    shift0 = jnp.zeros((Bp, D), F32)
    wkv0 = jnp.zeros((Bp, H, A_HEAD, A_HEAD), F32)
    vf_p = vf_s = None
    wkv_p, sh_p, wkv_s, sh_s = [], [], [], []

    for l in range(n_a):
        w_in = bf(a_w_in[l])
        lw_w = (bf(a_w1[l]), bf(a_w2[l]), vec(a_w0[l]))
        lora_a = (bf(a_a1[l]), bf(a_a2[l]), vec(a_a0[l]))
        lora_v = None if l == 0 else (bf(a_v1[l - 1]), bf(a_v2[l - 1]), vec(a_v0[l - 1]))
        k_k, k_a = vec(a_k_k[l]), vec(a_k_a[l])
        lnx_w, lnx_b, r_k = vec(a_lnx_w[l]), vec(a_lnx_b[l]), vec(a_r_k[l])
        w_out = bf(a_w_out[l])

        def layer(x, prev, s0, v_first, seq_len):
            mixed, hlast = _premix(x, prev, a_norm_w[l], a_mix[l], seq_len)
            rkvz = _bmm(mixed, w_in, F32, 4)
            outs = _lora(mixed, rkvz, v_first, lw_w, lora_a, lora_v, k_k, k_a, e_seg, et_seg)
            lw, am, bm, kp = outs[:4]
            vp = outs[4] if lora_v is not None else None
            v_use = rkvz[2] if vp is None else vp
            if seq_len == 1:
                y, s_fin = _wkv_tok(s0, rkvz[0], lw, kp, v_use, am, bm)
            else:
                y, s_fin = _wkv_seq(rkvz[0], kp, v_use, lw, am, bm, s0, seq_len)
            x_new = _rwkv_out(y, rkvz, kp, vp, x, lnx_w, lnx_b, r_k, e_seg, et_seg, w_out)
            return x_new, v_use, s_fin, hlast

        xp, v, S, last = layer(xp, shift0, wkv0, vf_p, Sp)
        wkv_p.append(S)
        sh_p.append(last)
        if l == 0:
            vf_p = v
        xs, v, S, last = layer(xs, state_shift[l], state_wkv[l], vf_s, 1)
        wkv_s.append(S)
        sh_s.append(last)
        if l == 0:
            vf_s = v

    pad = jnp.zeros((D, 128 - r_dim), F32)
    w_pe = w_dkv[:, c_lat:]
    w_dkv_ext = bf(jnp.concatenate([w_dkv[:, :c_lat], w_pe, pad, _rot_half_cols(w_pe, r_dim), pad], axis=1))
    cos_p, sin_p = _rope_tables(jnp.arange(Sp, dtype=jnp.int32), r_dim)
    cos_s, sin_s = _rope_tables(jnp.full((Bs,), past, dtype=jnp.int32), r_dim)
    ckv_p, kpe_p = _mla_kv(xp, vec(kv_in_norm_w), w_dkv_ext, vec(kv_norm_w), cos_p, sin_p, c_lat, r_dim, Sp)
    ckv_s, kpe_s = _mla_kv(xs, vec(kv_in_norm_w), w_dkv_ext, vec(kv_norm_w), cos_s, sin_s, c_lat, r_dim, 1)
    w_ukv = bf(jnp.concatenate([w_uk.reshape(c_lat, n_bh * d_nope), w_uv.reshape(c_lat, n_bh * d_v)], axis=1))
    k_cat, v_heads = _mla_expand(ckv_p, kpe_p, w_ukv, n_bh, d_nope, d_v)
    w_uk_t = bf(jnp.transpose(w_uk, (1, 2, 0)))
    w_uv_h = bf(jnp.transpose(w_uv, (1, 0, 2)))
    tile_h = lambda t: jnp.tile(t, (1, n_bh))
    cos_pq, sin_pq, cos_sq, sin_sq = tile_h(cos_p), tile_h(sin_p), tile_h(cos_s), tile_h(sin_s)

    y_p = y_s = None
    for l in range(n_b):
        last_layer = l == n_b - 1
        w_in = bf(b_w_in[l])
        n0 = n_bh * d_nope
        uq = b_w_uq[l].reshape(-1, n_bh, d_nope + r_dim)
        uq_pe = uq[:, :, d_nope:].reshape(-1, n_bh * r_dim)
        w_uq_ext = bf(jnp.concatenate([uq[:, :, :d_nope].reshape(-1, n0), uq_pe, _rot_half_cols(uq_pe, r_dim)], axis=1))
        w_out = bf(b_w_out[l])
        nw, qnw, fw = vec(b_norm_w[l]), vec(b_q_norm_w[l]), vec(final_norm_w)

        q_cat, z = _mla_q(xp, nw, w_in, qnw, w_uq_ext, cos_pq, sin_pq, n_bh, d_nope, r_dim, scale, Sp)
        o = _flash(q_cat, k_cat, v_heads, Sp)
        xp, y_p = _mla_out(o, z, xp, w_out, fw, last_layer)

        q_cat, z = _mla_q(xs, nw, w_in, qnw, w_uq_ext, cos_sq, sin_sq, n_bh, d_nope, r_dim, scale, 1)
        q_lat = _headwise_mm(q_cat[:, :, :d_nope], w_uk_t, BF16, "mla_absorb_q")
        o_lat = _decode_attn(jnp.transpose(q_lat, (1, 0, 2)), jnp.transpose(q_cat[:, :, d_nope:], (1, 0, 2)),
                             ckv_s.reshape(Bs, 1, c_lat), kpe_s.reshape(Bs, 1, r_dim),
                             cache_ckv, cache_kpe, page_table)
        o_h = _headwise_mm(jnp.transpose(o_lat, (1, 0, 2)), w_uv_h, F32, "mla_value_up")
        o = jnp.transpose(o_h, (1, 0, 2)).reshape(Bs, n_bh * d_v)
        xs, y_s = _mla_out(o, z, xs, w_out, fw, last_layer)

    return (y_p.reshape(Bp, Sp, D), y_s.reshape(Bs, Ts, D),
            jnp.stack(wkv_p), jnp.stack(sh_p), ckv_p.reshape(Bp, Sp, c_lat), kpe_p.reshape(Bp, Sp, r_dim),
            jnp.stack(wkv_s), jnp.stack(sh_s), ckv_s.reshape(Bs, Ts, c_lat), kpe_s.reshape(Bs, Ts, r_dim))
```

```python
import functools
import math

import jax
import jax.numpy as jnp
from jax import lax
from jax.experimental import pallas as pl
from jax.experimental.pallas import tpu as pltpu

F32 = jnp.float32
BF16 = jnp.bfloat16

NORM_EPS = 1e-6
ROPE_THETA = 10000.0
A_HEAD = 64
GN_EPS = A_HEAD * 1e-5
WKV_CHUNK = 64
WKV_GROUP_HEADS = 8
V7X_VMEM_LIMIT = 48 * 1024 * 1024

_NT = (((1,), (1,)), ((), ()))
_TN = (((0,), (0,)), ((), ()))


def _params(*sem):
    return pltpu.CompilerParams(dimension_semantics=sem, vmem_limit_bytes=V7X_VMEM_LIMIT)


def _dot(a, b):
    return jnp.dot(a.astype(BF16), b.astype(BF16), preferred_element_type=F32)


def _split2(x):
    hi = x.astype(BF16)
    lo = (x - hi.astype(F32)).astype(BF16)
    return hi, lo


def _split3(x):
    hi = x.astype(BF16)
    r = x - hi.astype(F32)
    mid = r.astype(BF16)
    lo = (r - mid.astype(F32)).astype(BF16)
    return hi, mid, lo


def _sigmoid(x):
    return 1.0 / (1.0 + jnp.exp(-x))


def _rms(x, w):
    return x * lax.rsqrt(jnp.mean(x * x, axis=-1, keepdims=True) + NORM_EPS) * w


def _premix_seq_body(seq_tiles, x_ref, prev_ref, nw_ref, mix_ref, mixed_ref, hlast_ref, carry_ref):
    i = pl.program_id(0)
    h = _rms(x_ref[...], nw_ref[...])
    tm = h.shape[0]
    prev_row = jnp.where(i % seq_tiles == 0, prev_ref[0], carry_ref[...])
    row = lax.broadcasted_iota(jnp.int32, h.shape, 0)
    shifted = jnp.where(row == 0, prev_row, pltpu.roll(h, 1, axis=0))
    last = h[tm - 1:tm, :]
    carry_ref[...] = last
    hlast_ref[0] = last
    dx = shifted - h
    for p in range(6):
        mixed_ref[p] = (h + dx * mix_ref[p:p + 1, :]).astype(BF16)


def _premix_tok_body(x_ref, prev_ref, nw_ref, mix_ref, mixed_ref, hlast_ref):
    h = _rms(x_ref[...], nw_ref[...])
    hlast_ref[...] = h
    dx = prev_ref[...] - h
    for p in range(6):
        mixed_ref[p] = (h + dx * mix_ref[p:p + 1, :]).astype(BF16)


def _premix(x, prev, norm_w, mix, seq_len):
    M, D = x.shape
    B = M // seq_len
    nw = norm_w.reshape(1, D)
    out_mixed = jax.ShapeDtypeStruct((6, M, D), BF16)
    if seq_len == 1:
        tm = min(M, 256)
        return pl.pallas_call(
            _premix_tok_body,
            grid=(M // tm,),
            in_specs=[pl.BlockSpec((tm, D), lambda i: (i, 0)),
                      pl.BlockSpec((tm, D), lambda i: (i, 0)),
                      pl.BlockSpec((1, D), lambda i: (0, 0)),
                      pl.BlockSpec((6, D), lambda i: (0, 0))],
            out_specs=[pl.BlockSpec((6, tm, D), lambda i: (0, i, 0)),
                       pl.BlockSpec((tm, D), lambda i: (i, 0))],
            out_shape=[out_mixed, jax.ShapeDtypeStruct((B, D), F32)],
            compiler_params=_params("arbitrary"),
            name="rwkv_premix_tok",
        )(x, prev, nw, mix)
    tm = min(seq_len, 256)
    seq_tiles = seq_len // tm
    mixed, hlast = pl.pallas_call(
        functools.partial(_premix_seq_body, seq_tiles),
        grid=(M // tm,),
        in_specs=[pl.BlockSpec((tm, D), lambda i: (i, 0)),
                  pl.BlockSpec((1, 1, D), lambda i: (i // seq_tiles, 0, 0)),
                  pl.BlockSpec((1, D), lambda i: (0, 0)),
                  pl.BlockSpec((6, D), lambda i: (0, 0))],
        out_specs=[pl.BlockSpec((6, tm, D), lambda i: (0, i, 0)),
                   pl.BlockSpec((1, 1, D), lambda i: (i // seq_tiles, 0, 0))],
        out_shape=[out_mixed, jax.ShapeDtypeStruct((B, 1, D), F32)],
        scratch_shapes=[pltpu.VMEM((1, D), F32)],
        compiler_params=_params("arbitrary"),
        name="rwkv_premix_seq",
    )(x, prev.reshape(B, 1, D), nw, mix)
    return mixed, hlast.reshape(B, D)


def _bmm_body(x_ref, w_ref, o_ref):
    o_ref[0] = jnp.dot(x_ref[0], w_ref[0], preferred_element_type=F32).astype(o_ref.dtype)


def _bmm(x, w, out_dtype, n_batch, tm=512, tn=2048):
    _, M, K = x.shape
    P, _, N = w.shape
    assert P == n_batch
    tm, tn = min(tm, M), min(tn, N)
    return pl.pallas_call(
        _bmm_body,
        grid=(P, N // tn, M // tm),
        in_specs=[pl.BlockSpec((1, tm, K), lambda p, n, m: (p, m, 0)),
                  pl.BlockSpec((1, K, tn), lambda p, n, m: (p, 0, n))],
        out_specs=pl.BlockSpec((1, tm, tn), lambda p, n, m: (p, m, n)),
        out_shape=jax.ShapeDtypeStruct((P, M, N), out_dtype),
        compiler_params=_params("arbitrary", "arbitrary", "arbitrary"),
        name="rwkv_in_proj",
    )(x, w)


def _lora_body(has_vres, *refs):
    if has_vres:
        (xw_ref, xa_ref, xv_ref, k_ref, v_ref, vf_ref,
         w1_ref, w2_ref, w0_ref, a1_ref, a2_ref, a0_ref, v1_ref, v2_ref, v0_ref,
         kk_ref, ka_ref, e_ref, et_ref,
         lw_ref, am_ref, bm_ref, kp_ref, vp_ref) = refs
    else:
        (xw_ref, xa_ref, k_ref,
         w1_ref, w2_ref, w0_ref, a1_ref, a2_ref, a0_ref,
         kk_ref, ka_ref, e_ref, et_ref,
         lw_ref, am_ref, bm_ref, kp_ref) = refs
    xd = w0_ref[...] + _dot(jnp.tanh(_dot(xw_ref[0], w1_ref[...])), w2_ref[...])
    lw_ref[...] = -math.exp(-0.5) * _sigmoid(xd)
    alr = _sigmoid(a0_ref[...] + _dot(_dot(xa_ref[0], a1_ref[...]), a2_ref[...]))
    k = k_ref[0]
    kk = k * kk_ref[...]
    sq_hi, sq_lo = _split2(kk * kk)
    e = e_ref[...]
    ss = jnp.dot(sq_hi, e, preferred_element_type=F32) + jnp.dot(sq_lo, e, preferred_element_type=F32)
    inv = 1.0 / jnp.maximum(jnp.sqrt(ss), 1e-12)
    et = et_ref[...]
    i_hi, i_mid, i_lo = _split3(inv)
    inv_b = (jnp.dot(i_hi, et, preferred_element_type=F32)
             + jnp.dot(i_mid, et, preferred_element_type=F32)
             + jnp.dot(i_lo, et, preferred_element_type=F32))
    kk = kk * inv_b
    am_ref[...] = -kk
    bm_ref[...] = kk * alr
    kp_ref[...] = k * (1.0 + (alr - 1.0) * ka_ref[...])
    if has_vres:
        v = v_ref[0]
        vg = _sigmoid(v0_ref[...] + _dot(_dot(xv_ref[0], v1_ref[...]), v2_ref[...]))
        vp_ref[...] = v + (vf_ref[0] - v) * vg


def _lora(mixed, rkvz, v_first, lw_w, lora_a, lora_v, k_k, k_a, e, et):
    _, M, D = mixed.shape
    DI = rkvz.shape[2]
    tm = min(M, 128)
    has_vres = lora_v is not None
    row = lambda i: (i, 0)
    full = lambda i: (0, 0)
    mixed_spec = lambda p: pl.BlockSpec((1, tm, D), lambda i: (p, i, 0))
    rkvz_spec = lambda p: pl.BlockSpec((1, tm, DI), lambda i: (p, i, 0))
    wspec = lambda a: pl.BlockSpec(a.shape, full)
    ins = [mixed, mixed]
    specs = [mixed_spec(4), mixed_spec(5)]
    if has_vres:
        ins += [mixed]
        specs += [mixed_spec(2)]
    ins += [rkvz]
    specs += [rkvz_spec(1)]
    if has_vres:
        vf, vf_slab = _slab(v_first)
        ins += [rkvz, vf]
        specs += [rkvz_spec(2), pl.BlockSpec((1, tm, DI), lambda i: (vf_slab, i, 0))]
    weights = list(lw_w) + list(lora_a) + (list(lora_v) if has_vres else []) + [k_k, k_a, e, et]
    ins += weights
    specs += [wspec(a) for a in weights]
    n_out = 5 if has_vres else 4
    return pl.pallas_call(
        functools.partial(_lora_body, has_vres),
        grid=(M // tm,),
        in_specs=specs,
        out_specs=[pl.BlockSpec((tm, DI), row)] * n_out,
        out_shape=[jax.ShapeDtypeStruct((M, DI), F32)] * n_out,
        compiler_params=_params("arbitrary"),
        name="rwkv_lora",
    )(*ins)


def _wkv_body(r_ref, k_ref, v_ref, lw_ref, a_ref, b_ref, s0_ref, y_ref, sfin_ref, s_scr):
    j = pl.program_id(2)
    C = WKV_CHUNK
    N = A_HEAD
    heads = range(s_scr.shape[0])
    n_chunks = r_ref.shape[1] // C

    @pl.when(j == 0)
    def _():
        s_scr[...] = s0_ref[0]

    row = lax.broadcasted_iota(jnp.int32, (C, C), 0)
    col = lax.broadcasted_iota(jnp.int32, (C, C), 1)
    strict = row > col
    tri = (row >= col).astype(BF16)
    eye = (row == col).astype(F32)
    row2 = lax.broadcasted_iota(jnp.int32, (C, 2 * C), 0)
    col2 = lax.broadcasted_iota(jnp.int32, (C, 2 * C), 1)
    incl2 = row2 >= jnp.where(col2 >= C, col2 - C, col2)

    def chunk(c, carry):
        rows = pl.ds(pl.multiple_of(c * C, C), C)
        lw = lw_ref[0, rows, :]
        l_hi, l_mid, l_lo = _split3(lw)
        cs = (jnp.dot(tri, l_hi, preferred_element_type=F32)
              + jnp.dot(tri, l_mid, preferred_element_type=F32)
              + jnp.dot(tri, l_lo, preferred_element_type=F32))
        cs_last = cs[C - 1:C, :]
        a, b, k, r, v = (ref[0, rows, :] for ref in (a_ref, b_ref, k_ref, r_ref, v_ref))
        e_neg = jnp.exp(-cs)
        e_rem = jnp.exp(cs_last - cs)
        at = (a * jnp.exp(cs - lw)).astype(BF16)
        rt = (r * jnp.exp(cs)).astype(BF16)
        bt = (b * e_neg).astype(BF16)
        kt = (k * e_neg).astype(BF16)
        bw = (b * e_rem).astype(BF16)
        kw = (k * e_rem).astype(BF16)
        vb = v.astype(BF16)
        w_end = jnp.exp(cs_last)
        sl = [slice(h * N, (h + 1) * N) for h in heads]
        ar = [jnp.concatenate([at[:, sl[h]], rt[:, sl[h]]], axis=0) for h in heads]
        bk = [jnp.concatenate([bt[:, sl[h]], kt[:, sl[h]]], axis=0) for h in heads]
        p = [lax.dot_general(ar[h], bk[h], _NT, preferred_element_type=F32) for h in heads]
        low = [jnp.where(strict, p[h][:C, :C], 0.0) for h in heads]
        m_ak = [jnp.where(strict, p[h][:C, C:], 0.0).astype(BF16) for h in heads]
        m_r = [jnp.where(incl2, p[h][C:, :], 0.0).astype(BF16) for h in heads]
        s_old = [s_scr[h] for h in heads]
        ay0 = [lax.dot_general(ar[h], s_old[h].astype(BF16), _NT, preferred_element_type=F32) for h in heads]
        r0 = [ay0[h][:C] + jnp.dot(m_ak[h], vb[:, sl[h]], preferred_element_type=F32) for h in heads]
        x = [_dot(low[h], low[h]) for h in heads]
        acc = [eye + low[h] for h in heads]
        span = 2
        while 2 * span < C:
            xa = [_dot(jnp.concatenate([x[h], acc[h]], axis=0), x[h]) for h in heads]
            x = [xa[h][:C] for h in heads]
            acc = [acc[h] + xa[h][C:] for h in heads]
            span *= 2
        inv = [acc[h] + _dot(acc[h], x[h]) for h in heads]
        u = [_dot(inv[h], r0[h]) for h in heads]
        uv = [jnp.concatenate([u[h].astype(BF16), vb[:, sl[h]]], axis=0) for h in heads]
        ys = [ay0[h][C:] + jnp.dot(m_r[h], uv[h], preferred_element_type=F32) for h in heads]
        bkw = [jnp.concatenate([bw[:, sl[h]], kw[:, sl[h]]], axis=0) for h in heads]
        for h in heads:
            s_scr[h] = s_old[h] * w_end[:, sl[h]] + lax.dot_general(uv[h], bkw[h], _TN, preferred_element_type=F32)
        y_ref[rows, :] = jnp.concatenate(ys, axis=1)
        return carry

    lax.fori_loop(0, n_chunks, chunk, 0)

    @pl.when(j == pl.num_programs(2) - 1)
    def _():
        sfin_ref[0] = s_scr[...]


def _slab(t):
    return t if isinstance(t, tuple) else (t.reshape((1,) + t.shape), 0)


def _wkv_seq(r, k, v, lw, a, b, s0, seq_len):
    srcs = [_slab(t) for t in (r, k, v, lw, a, b)]
    _, M, DI = srcs[0][0].shape
    B = M // seq_len
    H = DI // A_HEAD
    gh = min(WKV_GROUP_HEADS, H)
    gl = gh * A_HEAD
    tb = min(seq_len, 256)
    nt = seq_len // tb
    blk = pl.BlockSpec((tb, gl), lambda bi, g, j: (bi * nt + j, g))
    slab_blk = lambda p: pl.BlockSpec((1, tb, gl), lambda bi, g, j: (p, bi * nt + j, g))
    sblk = pl.BlockSpec((1, gh, A_HEAD, A_HEAD), lambda bi, g, j: (bi, g, 0, 0))
    r, k, v, lw, a, b = (t for t, _ in srcs)
    return pl.pallas_call(
        _wkv_body,
        grid=(B, H // gh, nt),
        in_specs=[slab_blk(p) for _, p in srcs] + [sblk],
        out_specs=[blk, sblk],
        out_shape=[jax.ShapeDtypeStruct((M, DI), F32), jax.ShapeDtypeStruct(s0.shape, F32)],
        scratch_shapes=[pltpu.VMEM((gh, A_HEAD, A_HEAD), F32)],
        compiler_params=_params("arbitrary", "arbitrary", "arbitrary"),
        name="rwkv_wkv_chunked",
    )(r, k, v, lw, a, b, s0)


def _wkv_tok_body(s_ref, r_ref, lw_ref, k_ref, vcol_ref, a_ref, b_ref, snew_ref, y_ref):
    s = s_ref[...]
    bc = lambda ref: ref[...][:, :, None, :]
    sa = jnp.sum(s * bc(a_ref), axis=-1, keepdims=True)
    s_new = s * jnp.exp(bc(lw_ref)) + sa * bc(b_ref) + vcol_ref[...] * bc(k_ref)
    snew_ref[...] = s_new
    y_ref[...] = jnp.sum(s_new * bc(r_ref), axis=-1, keepdims=True)


def _wkv_tok(s0, r, lw, k, v, a, b):
    B, H, N, _ = s0.shape
    bb = min(B, 2)
    heads = lambda t: t.reshape(B, H, N)
    vcol = jnp.broadcast_to(heads(v)[..., None], (B, H, N, N))
    sspec = pl.BlockSpec((bb, H, N, N), lambda i: (i, 0, 0, 0))
    vspec = pl.BlockSpec((bb, H, N), lambda i: (i, 0, 0))
    s_new, y = pl.pallas_call(
        _wkv_tok_body,
        grid=(B // bb,),
        in_specs=[sspec, vspec, vspec, vspec, sspec, vspec, vspec],
        out_specs=[sspec, pl.BlockSpec((bb, H, N, 1), lambda i: (i, 0, 0, 0))],
        out_shape=[jax.ShapeDtypeStruct(s0.shape, F32), jax.ShapeDtypeStruct((B, H, N, 1), F32)],
        compiler_params=_params("arbitrary"),
        name="rwkv_wkv_step",
    )(s0, heads(r), heads(lw), heads(k), vcol, heads(a), heads(b))
    return y.reshape(B, H * N), s_new


def _rwkv_out_body(y_ref, r_ref, kp_ref, vp_ref, z_ref, x_ref, lw_ref, lb_ref, rk_ref,
                   e_ref, et_ref, wo_ref, o_ref):
    n = float(A_HEAD)
    e = e_ref[...]
    et = et_ref[...]
    seg = lambda t: jnp.dot(t.astype(BF16), e, preferred_element_type=F32)

    def spread(t):
        hi, lo = _split2(t)
        return jnp.dot(hi, et, preferred_element_type=F32) + jnp.dot(lo, et, preferred_element_type=F32)

    y = y_ref[...]
    yc = y - spread(seg(y) / n)
    yc = yc - spread(seg(yc) / n)
    var = seg(yc * yc) / n
    yn = yc * spread(lax.rsqrt(var + GN_EPS)) * lw_ref[...] + lb_ref[...]
    vp = vp_ref[0] if len(vp_ref.shape) == 3 else vp_ref[...]
    bonus = spread(seg(r_ref[0] * kp_ref[...] * rk_ref[...])) * vp
    z = z_ref[0]
    g = (yn + bonus) * (z * _sigmoid(z))
    o_ref[...] = x_ref[...] + jnp.dot(g.astype(BF16), wo_ref[...], preferred_element_type=F32)


def _rwkv_out(y, rkvz, kp, vp, x, lnx_w, lnx_b, r_k, e, et, w_out):
    M, DI = y.shape
    D = x.shape[1]
    tm = min(M, 256)
    row = lambda i: (i, 0)
    full = lambda i: (0, 0)
    rkvz_spec = lambda p: pl.BlockSpec((1, tm, DI), lambda i: (p, i, 0))
    if vp is None:
        vp_in, vp_spec = rkvz, rkvz_spec(2)
    else:
        vp_in, vp_spec = vp, pl.BlockSpec((tm, DI), row)
    vec = pl.BlockSpec((1, DI), full)
    return pl.pallas_call(
        _rwkv_out_body,
        grid=(M // tm,),
        in_specs=[pl.BlockSpec((tm, DI), row), rkvz_spec(0), pl.BlockSpec((tm, DI), row), vp_spec,
                  rkvz_spec(3), pl.BlockSpec((tm, D), row), vec, vec, vec,
                  pl.BlockSpec(e.shape, full), pl.BlockSpec(et.shape, full),
                  pl.BlockSpec(w_out.shape, full)],
        out_specs=pl.BlockSpec((tm, D), row),
        out_shape=jax.ShapeDtypeStruct((M, D), F32),
        compiler_params=_params("arbitrary"),
        name="rwkv_out_proj",
    )(y, rkvz, kp, vp_in, rkvz, x, lnx_w, lnx_b, r_k, e, et, w_out)


def _mla_kv_body(c_lat, r_dim, x_ref, nw_ref, w_ref, kvn_ref, cos_ref, sin_ref, ckv_ref, kpe_ref):
    kv = _dot(_rms(x_ref[...], nw_ref[...]), w_ref[...])
    ckv_ref[...] = _rms(kv[:, :c_lat], kvn_ref[...])
    pe = kv[:, c_lat:c_lat + r_dim]
    pe_rot = kv[:, c_lat + 128:c_lat + 128 + r_dim]
    kpe_ref[...] = pe * cos_ref[...] + pe_rot * sin_ref[...]


def _mla_kv(x, norm_w, w_ext, kv_norm_w, cos, sin, c_lat, r_dim, seq_len):
    M, D = x.shape
    tm = min(seq_len if seq_len > 1 else M, 256)
    pos_tiles = max(seq_len // tm, 1)
    row = lambda i: (i, 0)
    full = lambda i: (0, 0)
    pos = lambda i: (i % pos_tiles, 0)
    return pl.pallas_call(
        functools.partial(_mla_kv_body, c_lat, r_dim),
        grid=(M // tm,),
        in_specs=[pl.BlockSpec((tm, D), row), pl.BlockSpec((1, D), full), pl.BlockSpec(w_ext.shape, full),
                  pl.BlockSpec((1, c_lat), full), pl.BlockSpec((tm, r_dim), pos), pl.BlockSpec((tm, r_dim), pos)],
        out_specs=[pl.BlockSpec((tm, c_lat), row), pl.BlockSpec((tm, r_dim), row)],
        out_shape=[jax.ShapeDtypeStruct((M, c_lat), F32), jax.ShapeDtypeStruct((M, r_dim), F32)],
        compiler_params=_params("arbitrary"),
        name="mla_shared_kv",
    )(x, norm_w, w_ext, kv_norm_w, cos, sin)


def _mla_expand_body(n_heads, d_nope, d_v, ckv_ref, kpe_ref, w_ref, k_ref, v_ref):
    kv = _dot(ckv_ref[...], w_ref[...]).astype(BF16)
    pe = kpe_ref[...].astype(BF16)
    v0 = n_heads * d_nope
    for h in range(n_heads):
        k_ref[h, :, :d_nope] = kv[:, h * d_nope:(h + 1) * d_nope]
        k_ref[h, :, d_nope:] = pe
        v_ref[h] = kv[:, v0 + h * d_v:v0 + (h + 1) * d_v]


def _mla_expand(ckv, kpe, w_ukv, n_heads, d_nope, d_v):
    M, c_lat = ckv.shape
    r_dim = kpe.shape[1]
    tm = min(M, 256)
    row = lambda i: (i, 0)
    return pl.pallas_call(
        functools.partial(_mla_expand_body, n_heads, d_nope, d_v),
        grid=(M // tm,),
        in_specs=[pl.BlockSpec((tm, c_lat), row), pl.BlockSpec((tm, r_dim), row),
                  pl.BlockSpec(w_ukv.shape, lambda i: (0, 0))],
        out_specs=[pl.BlockSpec((n_heads, tm, d_nope + r_dim), lambda i: (0, i, 0)),
                   pl.BlockSpec((n_heads, tm, d_v), lambda i: (0, i, 0))],
        out_shape=[jax.ShapeDtypeStruct((n_heads, M, d_nope + r_dim), BF16),
                   jax.ShapeDtypeStruct((n_heads, M, d_v), BF16)],
        compiler_params=_params("arbitrary"),
        name="mla_expand_kv",
    )(ckv, kpe, w_ukv)


def _mla_q_body(n_heads, d_nope, r_dim, q_lat, scale,
                x_ref, nw_ref, win_ref, qn_ref, wuq_ref, cos_ref, sin_ref, q_ref, z_ref):
    proj = _dot(_rms(x_ref[...], nw_ref[...]), win_ref[...])
    z_ref[...] = proj[:, q_lat:]
    q = _dot(_rms(proj[:, :q_lat], qn_ref[...]), wuq_ref[...]) * scale
    n0 = n_heads * d_nope
    n1 = n0 + n_heads * r_dim
    pe = q[:, n0:n1] * cos_ref[...] + q[:, n1:] * sin_ref[...]
    for h in range(n_heads):
        q_ref[h, :, :d_nope] = q[:, h * d_nope:(h + 1) * d_nope].astype(BF16)
        q_ref[h, :, d_nope:] = pe[:, h * r_dim:(h + 1) * r_dim].astype(BF16)


def _mla_q(x, norm_w, w_in, q_norm_w, w_uq_ext, cos_q, sin_q, n_heads, d_nope, r_dim, scale, seq_len):
    M, D = x.shape
    q_lat = q_norm_w.shape[1]
    d_gate = w_in.shape[1] - q_lat
    tm = min(seq_len if seq_len > 1 else M, 256)
    pos_tiles = max(seq_len // tm, 1)
    row = lambda i: (i, 0)
    full = lambda i: (0, 0)
    pos = lambda i: (i % pos_tiles, 0)
    return pl.pallas_call(
        functools.partial(_mla_q_body, n_heads, d_nope, r_dim, q_lat, scale),
        grid=(M // tm,),
        in_specs=[pl.BlockSpec((tm, D), row), pl.BlockSpec((1, D), full), pl.BlockSpec(w_in.shape, full),
                  pl.BlockSpec((1, q_lat), full), pl.BlockSpec(w_uq_ext.shape, full),
                  pl.BlockSpec((tm, n_heads * r_dim), pos), pl.BlockSpec((tm, n_heads * r_dim), pos)],
        out_specs=[pl.BlockSpec((n_heads, tm, d_nope + r_dim), lambda i: (0, i, 0)),
                   pl.BlockSpec((tm, d_gate), row)],
        out_shape=[jax.ShapeDtypeStruct((n_heads, M, d_nope + r_dim), BF16),
                   jax.ShapeDtypeStruct((M, d_gate), F32)],
        compiler_params=_params("arbitrary"),
        name="mla_query",
    )(x, norm_w, w_in, q_norm_w, w_uq_ext, cos_q, sin_q)


def _flash_body(q_ref, k_ref, v_ref, o_ref):
    qi = pl.program_id(2)
    t = q_ref.shape[1]
    causal = (lax.broadcasted_iota(jnp.int32, (t, t), 1) <= lax.broadcasted_iota(jnp.int32, (t, t), 0))

    def attend(n):
        q = q_ref[0]
        d0 = n * t
        s_diag = lax.dot_general(q, k_ref[0, d0:d0 + t, :], _NT, preferred_element_type=F32)
        s_diag = jnp.where(causal, s_diag, -jnp.inf)
        m = jnp.max(s_diag, axis=-1, keepdims=True)
        if n > 0:
            s_past = lax.dot_general(q, k_ref[0, :d0, :], _NT, preferred_element_type=F32)
            m = jnp.maximum(m, jnp.max(s_past, axis=-1, keepdims=True))
        p_diag = jnp.exp(s_diag - m)
        l = jnp.sum(p_diag, axis=-1, keepdims=True)
        o = jnp.dot(p_diag.astype(BF16), v_ref[0, d0:d0 + t, :], preferred_element_type=F32)
        if n > 0:
            p_past = jnp.exp(s_past - m)
            l = l + jnp.sum(p_past, axis=-1, keepdims=True)
            o = o + jnp.dot(p_past.astype(BF16), v_ref[0, :d0, :], preferred_element_type=F32)
        o_ref[...] = o / l

    for n in range(k_ref.shape[1] // t):
        pl.when(qi == n)(functools.partial(attend, n))


def _flash(q, k, v, seq_len):
    H, M, dk = q.shape
    dv = v.shape[2]
    B = M // seq_len
    t = min(seq_len, 512)
    nb = seq_len // t
    return pl.pallas_call(
        _flash_body,
        grid=(B, H, nb),
        in_specs=[pl.BlockSpec((1, t, dk), lambda b, h, qi: (h, b * nb + qi, 0)),
                  pl.BlockSpec((1, seq_len, dk), lambda b, h, qi: (h, b, 0)),
                  pl.BlockSpec((1, seq_len, dv), lambda b, h, qi: (h, b, 0))],
        out_specs=pl.BlockSpec((t, dv), lambda b, h, qi: (b * nb + qi, h)),
        out_shape=jax.ShapeDtypeStruct((M, H * dv), F32),
        compiler_params=_params("arbitrary", "arbitrary", "arbitrary"),
        name="mla_prompt_attention",
    )(q, k, v)


def _mla_out_body(with_norm, o_ref, z_ref, x_ref, wo_ref, fw_ref, xo_ref, *maybe_y):
    z = z_ref[...]
    g = o_ref[...] * (z * _sigmoid(z))
    xn = x_ref[...] + jnp.dot(g.astype(BF16), wo_ref[...], preferred_element_type=F32)
    xo_ref[...] = xn
    if with_norm:
        maybe_y[0][...] = _rms(xn, fw_ref[...])


def _mla_out(o, z, x, w_out, final_w, with_norm):
    M, DI = o.shape
    D = x.shape[1]
    tm = min(M, 256)
    row = lambda i: (i, 0)
    full = lambda i: (0, 0)
    n_out = 2 if with_norm else 1
    outs = pl.pallas_call(
        functools.partial(_mla_out_body, with_norm),
        grid=(M // tm,),
        in_specs=[pl.BlockSpec((tm, DI), row), pl.BlockSpec((tm, DI), row), pl.BlockSpec((tm, D), row),
                  pl.BlockSpec(w_out.shape, full), pl.BlockSpec((1, D), full)],
        out_specs=[pl.BlockSpec((tm, D), row)] * n_out,
        out_shape=[jax.ShapeDtypeStruct((M, D), F32)] * n_out,
        compiler_params=_params("arbitrary"),
        name="mla_out_proj",
    )(o, z, x, w_out, final_w)
    return outs if with_norm else (outs[0], None)


def _headwise_mm_body(x_ref, w_ref, o_ref):
    o_ref[0] = jnp.dot(x_ref[0].astype(BF16), w_ref[0], preferred_element_type=F32).astype(o_ref.dtype)


def _headwise_mm(x, w, out_dtype, name):
    H, B, K = x.shape
    N = w.shape[2]
    return pl.pallas_call(
        _headwise_mm_body,
        grid=(H,),
        in_specs=[pl.BlockSpec((1, B, K), lambda h: (h, 0, 0)), pl.BlockSpec((1, K, N), lambda h: (h, 0, 0))],
        out_specs=pl.BlockSpec((1, B, N), lambda h: (h, 0, 0)),
        out_shape=jax.ShapeDtypeStruct((H, B, N), out_dtype),
        compiler_params=_params("arbitrary"),
        name=name,
    )(x, w)


DECODE_PAGES_PER_STEP = 32


def _decode_body(n_pg, pt_ref, ql_ref, qp_ref, cn_ref, pn_ref, *refs):
    ckv_refs = refs[:n_pg]
    kpe_refs = refs[n_pg:2 * n_pg]
    o_ref, m_scr, l_scr, acc_scr = refs[2 * n_pg:]
    j = pl.program_id(1)
    ql = ql_ref[0]
    qp = qp_ref[0]

    @pl.when(j == 0)
    def _():
        cn = cn_ref[0]
        s_new = (jnp.sum(ql.astype(F32) * cn, axis=-1, keepdims=True)
                 + jnp.sum(qp.astype(F32) * pn_ref[0], axis=-1, keepdims=True))
        m_scr[...] = jnp.broadcast_to(s_new, m_scr.shape)
        l_scr[...] = jnp.ones(l_scr.shape, F32)
        acc_scr[...] = jnp.broadcast_to(cn, acc_scr.shape)

    ckv = jnp.concatenate([r[0].astype(BF16) for r in ckv_refs], axis=0)
    kpe_t = jnp.concatenate([r[0].astype(BF16) for r in kpe_refs], axis=1)
    s = (lax.dot_general(ql, ckv, _NT, preferred_element_type=F32)
         + jnp.dot(qp, kpe_t, preferred_element_type=F32))
    m_old = m_scr[...]
    m_new = jnp.maximum(m_old, jnp.max(s, axis=-1, keepdims=True))
    alpha = jnp.exp(m_old - m_new)
    p = jnp.exp(s - m_new[:, :1])
    l_new = alpha * l_scr[...] + jnp.sum(p, axis=-1, keepdims=True)
    acc = alpha[:, :1] * acc_scr[...] + jnp.dot(p.astype(BF16), ckv, preferred_element_type=F32)
    m_scr[...] = m_new
    l_scr[...] = l_new
    acc_scr[...] = acc

    @pl.when(j == pl.num_programs(1) - 1)
    def _():
        o_ref[0] = acc / l_new[:, :1]


def _decode_attn(q_lat, q_pe, ckv_new, kpe_new, cache_ckv, cache_kpe, page_table):
    B, H, C = q_lat.shape
    R = q_pe.shape[2]
    page = cache_ckv.shape[1]
    n_pages = page_table.shape[1]
    n_pg = min(DECODE_PAGES_PER_STEP, n_pages)
    ckv_spec = lambda i: pl.BlockSpec((1, page, C), lambda b, j, pt: (pt[b, j * n_pg + i], 0, 0))
    kpe_spec = lambda i: pl.BlockSpec((1, R, page), lambda b, j, pt: (pt[b, j * n_pg + i], 0, 0))
    per_b = lambda b, j, pt: (b, 0, 0)
    grid_spec = pltpu.PrefetchScalarGridSpec(
        num_scalar_prefetch=1,
        grid=(B, n_pages // n_pg),
        in_specs=[pl.BlockSpec((1, H, C), per_b), pl.BlockSpec((1, H, R), per_b),
                  pl.BlockSpec((1, 1, C), per_b), pl.BlockSpec((1, 1, R), per_b)]
                 + [ckv_spec(i) for i in range(n_pg)] + [kpe_spec(i) for i in range(n_pg)],
        out_specs=pl.BlockSpec((1, H, C), per_b),
        scratch_shapes=[pltpu.VMEM((H, 128), F32), pltpu.VMEM((H, 128), F32), pltpu.VMEM((H, C), F32)],
    )
    return pl.pallas_call(
        functools.partial(_decode_body, n_pg),
        grid_spec=grid_spec,
        out_shape=jax.ShapeDtypeStruct((B, H, C), F32),
        compiler_params=_params("arbitrary", "arbitrary"),
        name="mla_sample_attention",
    )(page_table, q_lat, q_pe, ckv_new, kpe_new, *([cache_ckv] * n_pg), *([cache_kpe] * n_pg))


def _rot_half_cols(w, r_dim):
    lead = w.shape[:-1]
    wb = w.reshape(lead + (-1, 2, r_dim // 2))
    return jnp.stack([-wb[..., 1, :], wb[..., 0, :]], axis=-2).reshape(w.shape)


def _rope_tables(pos, r_dim):
    half = r_dim // 2
    inv_freq = ROPE_THETA ** (-jnp.arange(half, dtype=F32) / half)
    ang = pos.astype(F32)[:, None] * inv_freq[None, :]
    cos, sin = jnp.cos(ang), jnp.sin(ang)
    return jnp.concatenate([cos, cos], axis=1), jnp.concatenate([sin, sin], axis=1)


def kernel(x_prompt, x_sample, state_wkv, state_shift, cache_ckv, cache_kpe, page_table, a_norm_w, a_mix, a_w_in, a_w0, a_w1, a_w2, a_a0, a_a1, a_a2, a_v0, a_v1, a_v2, a_k_k, a_k_a, a_r_k, a_lnx_w, a_lnx_b, a_w_out, kv_in_norm_w, w_dkv, kv_norm_w, w_uk, w_uv, b_norm_w, b_w_in, b_q_norm_w, b_w_uq, b_w_out, final_norm_w):
    Bp, Sp, D = x_prompt.shape
    Bs, Ts, _ = x_sample.shape
    assert Ts == 1
    n_a = a_norm_w.shape[0]
    n_b = b_norm_w.shape[0]
    DI = a_w_in.shape[3]
    H = DI // A_HEAD
    c_lat, n_bh, d_nope = w_uk.shape
    d_v = w_uv.shape[2]
    r_dim = cache_kpe.shape[2]
    past = page_table.shape[1] * cache_ckv.shape[1]
    scale = float(d_nope + r_dim) ** -0.5
    bf = lambda t: t.astype(BF16)
    vec = lambda t: t.reshape(1, -1)

    head_of_lane = jnp.arange(DI, dtype=jnp.int32) // A_HEAD
    e_seg = (head_of_lane[:, None] == jnp.arange(H, dtype=jnp.int32)[None, :]).astype(BF16)
    et_seg = e_seg.T

    xp = x_prompt.reshape(Bp * Sp, D)
    xs = x_sample.reshape(Bs, D)
    shift0 = jnp.zeros((Bp, D), F32)
    wkv0 = jnp.zeros((Bp, H, A_HEAD, A_HEAD), F32)
    vf_p = vf_s = None
    wkv_p, sh_p, wkv_s, sh_s = [], [], [], []

    for l in range(n_a):
        w_in = bf(a_w_in[l])
        lw_w = (bf(a_w1[l]), bf(a_w2[l]), vec(a_w0[l]))
        lora_a = (bf(a_a1[l]), bf(a_a2[l]), vec(a_a0[l]))
        lora_v = None if l == 0 else (bf(a_v1[l - 1]), bf(a_v2[l - 1]), vec(a_v0[l - 1]))
        k_k, k_a = vec(a_k_k[l]), vec(a_k_a[l])
        lnx_w, lnx_b, r_k = vec(a_lnx_w[l]), vec(a_lnx_b[l]), vec(a_r_k[l])
        w_out = bf(a_w_out[l])

        def layer(x, prev, s0, v_first, seq_len):
            mixed, hlast = _premix(x, prev, a_norm_w[l], a_mix[l], seq_len)
            rkvz = _bmm(mixed, w_in, F32, 4)
            outs = _lora(mixed, rkvz, v_first, lw_w, lora_a, lora_v, k_k, k_a, e_seg, et_seg)
            lw, am, bm, kp = outs[:4]
            vp = outs[4] if lora_v is not None else None
            if seq_len == 1:
                y, s_fin = _wkv_tok(s0, rkvz[0], lw, kp, rkvz[2] if vp is None else vp, am, bm)
            else:
                y, s_fin = _wkv_seq((rkvz, 0), kp, (rkvz, 2) if vp is None else vp, lw, am, bm, s0, seq_len)
            x_new = _rwkv_out(y, rkvz, kp, vp, x, lnx_w, lnx_b, r_k, e_seg, et_seg, w_out)
            return x_new, (rkvz, 2), s_fin, hlast

        xp, v, S, last = layer(xp, shift0, wkv0, vf_p, Sp)
        wkv_p.append(S)
        sh_p.append(last)
        if l == 0:
            vf_p = v
        xs, v, S, last = layer(xs, state_shift[l], state_wkv[l], vf_s, 1)
        wkv_s.append(S)
        sh_s.append(last)
        if l == 0:
            vf_s = v

    pad = jnp.zeros((D, 128 - r_dim), F32)
    w_pe = w_dkv[:, c_lat:]
    w_dkv_ext = bf(jnp.concatenate([w_dkv[:, :c_lat], w_pe, pad, _rot_half_cols(w_pe, r_dim), pad], axis=1))
    cos_p, sin_p = _rope_tables(jnp.arange(Sp, dtype=jnp.int32), r_dim)
    cos_s, sin_s = _rope_tables(jnp.full((Bs,), past, dtype=jnp.int32), r_dim)
    ckv_p, kpe_p = _mla_kv(xp, vec(kv_in_norm_w), w_dkv_ext, vec(kv_norm_w), cos_p, sin_p, c_lat, r_dim, Sp)
    ckv_s, kpe_s = _mla_kv(xs, vec(kv_in_norm_w), w_dkv_ext, vec(kv_norm_w), cos_s, sin_s, c_lat, r_dim, 1)
    w_ukv = bf(jnp.concatenate([w_uk.reshape(c_lat, n_bh * d_nope), w_uv.reshape(c_lat, n_bh * d_v)], axis=1))
    k_cat, v_heads = _mla_expand(ckv_p, kpe_p, w_ukv, n_bh, d_nope, d_v)
    w_uk_t = bf(jnp.transpose(w_uk, (1, 2, 0)))
    w_uv_h = bf(jnp.transpose(w_uv, (1, 0, 2)))
    cache_kpe_t = jnp.swapaxes(cache_kpe, 1, 2)
    tile_h = lambda t: jnp.tile(t, (1, n_bh))
    cos_pq, sin_pq, cos_sq, sin_sq = tile_h(cos_p), tile_h(sin_p), tile_h(cos_s), tile_h(sin_s)

    y_p = y_s = None
    for l in range(n_b):
        last_layer = l == n_b - 1
        w_in = bf(b_w_in[l])
        n0 = n_bh * d_nope
        uq = b_w_uq[l].reshape(-1, n_bh, d_nope + r_dim)
        uq_pe = uq[:, :, d_nope:].reshape(-1, n_bh * r_dim)
        w_uq_ext = bf(jnp.concatenate([uq[:, :, :d_nope].reshape(-1, n0), uq_pe, _rot_half_cols(uq_pe, r_dim)], axis=1))
        w_out = bf(b_w_out[l])
        nw, qnw, fw = vec(b_norm_w[l]), vec(b_q_norm_w[l]), vec(final_norm_w)

        q_cat, z = _mla_q(xp, nw, w_in, qnw, w_uq_ext, cos_pq, sin_pq, n_bh, d_nope, r_dim, scale, Sp)
        o = _flash(q_cat, k_cat, v_heads, Sp)
        xp, y_p = _mla_out(o, z, xp, w_out, fw, last_layer)

        q_cat, z = _mla_q(xs, nw, w_in, qnw, w_uq_ext, cos_sq, sin_sq, n_bh, d_nope, r_dim, scale, 1)
        q_lat = _headwise_mm(q_cat[:, :, :d_nope], w_uk_t, BF16, "mla_absorb_q")
        o_lat = _decode_attn(jnp.transpose(q_lat, (1, 0, 2)), jnp.transpose(q_cat[:, :, d_nope:], (1, 0, 2)),
                             ckv_s.reshape(Bs, 1, c_lat), kpe_s.reshape(Bs, 1, r_dim),
                             cache_ckv, cache_kpe_t, page_table)
        o_h = _headwise_mm(jnp.transpose(o_lat, (1, 0, 2)), w_uv_h, F32, "mla_value_up")
        o = jnp.transpose(o_h, (1, 0, 2)).reshape(Bs, n_bh * d_v)
        xs, y_s = _mla_out(o, z, xs, w_out, fw, last_layer)

    return (y_p.reshape(Bp, Sp, D), y_s.reshape(Bs, Ts, D),
            jnp.stack(wkv_p), jnp.stack(sh_p), ckv_p.reshape(Bp, Sp, c_lat), kpe_p.reshape(Bp, Sp, r_dim),
            jnp.stack(wkv_s), jnp.stack(sh_s), ckv_s.reshape(Bs, Ts, c_lat), kpe_s.reshape(Bs, Ts, r_dim))
```

```python
import functools
import math

import jax
import jax.numpy as jnp
from jax import lax
from jax.experimental import pallas as pl
from jax.experimental.pallas import tpu as pltpu

F32 = jnp.float32
BF16 = jnp.bfloat16

NORM_EPS = 1e-6
ROPE_THETA = 10000.0
A_HEAD = 64
GN_EPS = A_HEAD * 1e-5
WKV_CHUNK = 64
WKV_LANE_HEADS = 4
HEAD_SUM_WIDTH = 256
FLASH_BLOCK = 512
V7X_VMEM_LIMIT = 48 * 1024 * 1024

_NT = (((1,), (1,)), ((), ()))
_TN = (((0,), (0,)), ((), ()))


def _params(*sem):
    return pltpu.CompilerParams(dimension_semantics=sem, vmem_limit_bytes=V7X_VMEM_LIMIT)


def _dot(a, b):
    return jnp.dot(a.astype(BF16), b.astype(BF16), preferred_element_type=F32)


def _split2(x):
    hi = x.astype(BF16)
    lo = (x - hi.astype(F32)).astype(BF16)
    return hi, lo


def _split3(x):
    hi = x.astype(BF16)
    r = x - hi.astype(F32)
    mid = r.astype(BF16)
    lo = (r - mid.astype(F32)).astype(BF16)
    return hi, mid, lo


def _head_sums(t, ones_bd, two_pass):
    w = ones_bd.shape[0]
    cols = []
    for i in range(t.shape[1] // w):
        blk = t[:, i * w:(i + 1) * w]
        if two_pass:
            hi, lo = _split2(blk)
            cols.append(jnp.dot(hi, ones_bd, preferred_element_type=F32)
                        + jnp.dot(lo, ones_bd, preferred_element_type=F32))
        else:
            cols.append(jnp.dot(blk.astype(BF16), ones_bd, preferred_element_type=F32))
    return jnp.concatenate(cols, axis=1)


def _sigmoid(x):
    return 1.0 / (1.0 + jnp.exp(-x))


def _rms(x, w):
    return x * lax.rsqrt(jnp.mean(x * x, axis=-1, keepdims=True) + NORM_EPS) * w


def _premix_seq_body(seq_tiles, x_ref, prev_ref, nw_ref, mix_ref, mixed_ref, hlast_ref, carry_ref):
    i = pl.program_id(0)
    h = _rms(x_ref[...], nw_ref[...])
    tm = h.shape[0]
    prev_row = jnp.where(i % seq_tiles == 0, prev_ref[0], carry_ref[...])
    row = lax.broadcasted_iota(jnp.int32, h.shape, 0)
    shifted = jnp.where(row == 0, prev_row, pltpu.roll(h, 1, axis=0))
    last = h[tm - 1:tm, :]
    carry_ref[...] = last
    hlast_ref[0] = last
    dx = shifted - h
    for p in range(6):
        mixed_ref[p] = (h + dx * mix_ref[p:p + 1, :]).astype(BF16)


def _premix_tok_body(x_ref, prev_ref, nw_ref, mix_ref, mixed_ref, hlast_ref):
    h = _rms(x_ref[...], nw_ref[...])
    hlast_ref[...] = h
    dx = prev_ref[...] - h
    for p in range(6):
        mixed_ref[p] = (h + dx * mix_ref[p:p + 1, :]).astype(BF16)


def _premix(x, prev, norm_w, mix, seq_len):
    M, D = x.shape
    B = M // seq_len
    nw = norm_w.reshape(1, D)
    out_mixed = jax.ShapeDtypeStruct((6, M, D), BF16)
    if seq_len == 1:
        tm = min(M, 256)
        return pl.pallas_call(
            _premix_tok_body,
            grid=(M // tm,),
            in_specs=[pl.BlockSpec((tm, D), lambda i: (i, 0)),
                      pl.BlockSpec((tm, D), lambda i: (i, 0)),
                      pl.BlockSpec((1, D), lambda i: (0, 0)),
                      pl.BlockSpec((6, D), lambda i: (0, 0))],
            out_specs=[pl.BlockSpec((6, tm, D), lambda i: (0, i, 0)),
                       pl.BlockSpec((tm, D), lambda i: (i, 0))],
            out_shape=[out_mixed, jax.ShapeDtypeStruct((B, D), F32)],
            compiler_params=_params("arbitrary"),
            name="rwkv_premix_tok",
        )(x, prev, nw, mix)
    tm = min(seq_len, 256)
    seq_tiles = seq_len // tm
    mixed, hlast = pl.pallas_call(
        functools.partial(_premix_seq_body, seq_tiles),
        grid=(M // tm,),
        in_specs=[pl.BlockSpec((tm, D), lambda i: (i, 0)),
                  pl.BlockSpec((1, 1, D), lambda i: (i // seq_tiles, 0, 0)),
                  pl.BlockSpec((1, D), lambda i: (0, 0)),
                  pl.BlockSpec((6, D), lambda i: (0, 0))],
        out_specs=[pl.BlockSpec((6, tm, D), lambda i: (0, i, 0)),
                   pl.BlockSpec((1, 1, D), lambda i: (i // seq_tiles, 0, 0))],
        out_shape=[out_mixed, jax.ShapeDtypeStruct((B, 1, D), F32)],
        scratch_shapes=[pltpu.VMEM((1, D), F32)],
        compiler_params=_params("arbitrary"),
        name="rwkv_premix_seq",
    )(x, prev.reshape(B, 1, D), nw, mix)
    return mixed, hlast.reshape(B, D)


def _bmm_body(x_ref, w_ref, o_ref):
    o_ref[0] = jnp.dot(x_ref[0], w_ref[0], preferred_element_type=F32).astype(o_ref.dtype)


def _bmm(x, w, out_dtype, n_batch, tm=512, tn=2048):
    _, M, K = x.shape
    P, _, N = w.shape
    assert P == n_batch
    tm, tn = min(tm, M), min(tn, N)
    return pl.pallas_call(
        _bmm_body,
        grid=(P, N // tn, M // tm),
        in_specs=[pl.BlockSpec((1, tm, K), lambda p, n, m: (p, m, 0)),
                  pl.BlockSpec((1, K, tn), lambda p, n, m: (p, 0, n))],
        out_specs=pl.BlockSpec((1, tm, tn), lambda p, n, m: (p, m, n)),
        out_shape=jax.ShapeDtypeStruct((P, M, N), out_dtype),
        compiler_params=_params("arbitrary", "arbitrary", "arbitrary"),
        name="rwkv_in_proj",
    )(x, w)


def _lora_body(has_vres, *refs):
    if has_vres:
        (xw_ref, xa_ref, xv_ref, k_ref, v_ref, vf_ref,
         w1_ref, w2_ref, w0_ref, a1_ref, a2_ref, a0_ref, v1_ref, v2_ref, v0_ref,
         kk_ref, ka_ref, ones_ref,
         lw_ref, am_ref, bm_ref, kp_ref, vp_ref) = refs
    else:
        (xw_ref, xa_ref, k_ref,
         w1_ref, w2_ref, w0_ref, a1_ref, a2_ref, a0_ref,
         kk_ref, ka_ref, ones_ref,
         lw_ref, am_ref, bm_ref, kp_ref) = refs
    xd = w0_ref[...] + _dot(jnp.tanh(_dot(xw_ref[0], w1_ref[...])), w2_ref[...])
    lw_ref[...] = -math.exp(-0.5) * _sigmoid(xd)
    alr = _sigmoid(a0_ref[...] + _dot(_dot(xa_ref[0], a1_ref[...]), a2_ref[...]))
    k = k_ref[0]
    kk = k * kk_ref[...]
    kk = kk / jnp.maximum(jnp.sqrt(_head_sums(kk * kk, ones_ref[...], True)), 1e-12)
    am_ref[...] = -kk
    bm_ref[...] = kk * alr
    kp_ref[...] = k * (1.0 + (alr - 1.0) * ka_ref[...])
    if has_vres:
        v = v_ref[0]
        vg = _sigmoid(v0_ref[...] + _dot(_dot(xv_ref[0], v1_ref[...]), v2_ref[...]))
        vp_ref[...] = v + (vf_ref[0] - v) * vg


def _lora(mixed, rkvz, v_first, lw_w, lora_a, lora_v, k_k, k_a, ones_bd):
    _, M, D = mixed.shape
    DI = rkvz.shape[2]
    tm = min(M, 128)
    has_vres = lora_v is not None
    row = lambda i: (i, 0)
    full = lambda i: (0, 0)
    mixed_spec = lambda p: pl.BlockSpec((1, tm, D), lambda i: (p, i, 0))
    rkvz_spec = lambda p: pl.BlockSpec((1, tm, DI), lambda i: (p, i, 0))
    wspec = lambda a: pl.BlockSpec(a.shape, full)
    ins = [mixed, mixed]
    specs = [mixed_spec(4), mixed_spec(5)]
    if has_vres:
        ins += [mixed]
        specs += [mixed_spec(2)]
    ins += [rkvz]
    specs += [rkvz_spec(1)]
    if has_vres:
        vf, vf_slab = _slab(v_first)
        ins += [rkvz, vf]
        specs += [rkvz_spec(2), pl.BlockSpec((1, tm, DI), lambda i: (vf_slab, i, 0))]
    weights = list(lw_w) + list(lora_a) + (list(lora_v) if has_vres else []) + [k_k, k_a, ones_bd]
    ins += weights
    specs += [wspec(a) for a in weights]
    n_out = 5 if has_vres else 4
    return pl.pallas_call(
        functools.partial(_lora_body, has_vres),
        grid=(M // tm,),
        in_specs=specs,
        out_specs=[pl.BlockSpec((tm, DI), row)] * n_out,
        out_shape=[jax.ShapeDtypeStruct((M, DI), F32)] * n_out,
        compiler_params=_params("arbitrary"),
        name="rwkv_lora",
    )(*ins)


def _wkv_body(r_ref, k_ref, v_ref, lw_ref, a_ref, b_ref, s0_ref, y_ref, sfin_ref, s_scr):
    j = pl.program_id(1)
    C = WKV_CHUNK
    N = A_HEAD
    GH = WKV_LANE_HEADS
    GL = GH * N
    assert C == N
    shift = N.bit_length() - 1
    groups = range(s_scr.shape[0])
    n_chunks = r_ref.shape[1] // C

    @pl.when(j == 0)
    def _():
        for g in groups:
            s_scr[g] = jnp.concatenate([s0_ref[0, g * GH + h] for h in range(GH)], axis=1)

    row = lax.broadcasted_iota(jnp.int32, (C, GL), 0)
    lane = lax.broadcasted_iota(jnp.int32, (C, GL), 1)
    lane_in_head = jnp.bitwise_and(lane, N - 1)
    lane_head = jnp.right_shift(lane, shift)
    strict = row > lane_in_head
    incl = (row >= lane_in_head).astype(F32)
    incl2 = jnp.concatenate([incl, incl], axis=1) > 0.5
    eye = (row == lane_in_head).astype(F32)
    tri = (lax.broadcasted_iota(jnp.int32, (C, C), 0) >= lax.broadcasted_iota(jnp.int32, (C, C), 1)).astype(BF16)
    blockdiag = (jnp.right_shift(lax.broadcasted_iota(jnp.int32, (GL, GL), 0), shift)
                 == jnp.right_shift(lax.broadcasted_iota(jnp.int32, (GL, GL), 1), shift)).astype(BF16)

    def bd(x):
        return jnp.concatenate([x.astype(BF16)] * GH, axis=0) * blockdiag

    def dotf(a, b):
        return jnp.dot(a, b, preferred_element_type=F32)

    def chunk(c, carry):
        rows = pl.ds(pl.multiple_of(c * C, C), C)
        lw = lw_ref[0, rows, :]
        l_hi, l_lo = _split2(lw)
        cs = dotf(tri, l_hi) + dotf(tri, l_lo)
        cs_last = cs[C - 1:C, :]
        a, b, k, r, v = (ref[0, rows, :] for ref in (a_ref, b_ref, k_ref, r_ref, v_ref))
        e_neg = jnp.exp(-cs)
        e_rem = jnp.exp(cs_last - cs)
        at = (a * jnp.exp(cs - lw)).astype(BF16)
        rt = (r * jnp.exp(cs)).astype(BF16)
        bt = (b * e_neg).astype(BF16)
        kt = (k * e_neg).astype(BF16)
        bw = (b * e_rem).astype(BF16)
        kw = (k * e_rem).astype(BF16)
        vb = v.astype(BF16)
        w_end = jnp.exp(cs_last)
        gl = [slice(g * GL, (g + 1) * GL) for g in groups]
        ar = [jnp.concatenate([at[:, gl[g]], rt[:, gl[g]]], axis=0) for g in groups]
        bk = [jnp.concatenate([bd(bt[:, gl[g]]), bd(kt[:, gl[g]])], axis=0) for g in groups]
        p = [lax.dot_general(ar[g], bk[g], _NT, preferred_element_type=F32) for g in groups]
        low = [jnp.where(strict, p[g][:C, :GL], 0.0) for g in groups]
        m_ak = [jnp.where(strict, p[g][:C, GL:], 0.0).astype(BF16) for g in groups]
        m_r = [jnp.where(incl2, p[g][C:, :], 0.0).astype(BF16) for g in groups]
        s_old = [s_scr[g] for g in groups]
        ay0 = [lax.dot_general(ar[g], bd(s_old[g]), _NT, preferred_element_type=F32) for g in groups]
        bdv = [bd(vb[:, gl[g]]) for g in groups]
        r0 = [ay0[g][:C] + dotf(m_ak[g], bdv[g]) for g in groups]
        x = [dotf(low[g].astype(BF16), bd(low[g])) for g in groups]
        acc = [eye + low[g] for g in groups]
        span = 2
        while 2 * span < C:
            xa = [dotf(jnp.concatenate([x[g], acc[g]], axis=0).astype(BF16), bd(x[g])) for g in groups]
            x = [xa[g][:C] for g in groups]
            acc = [acc[g] + xa[g][C:] for g in groups]
            span *= 2
        inv = [acc[g] + dotf(acc[g].astype(BF16), bd(x[g])) for g in groups]
        ub = [dotf(inv[g].astype(BF16), bd(r0[g])).astype(BF16) for g in groups]
        for g in groups:
            y_ref[rows, gl[g]] = ay0[g][C:] + dotf(m_r[g], jnp.concatenate([bd(ub[g]), bdv[g]], axis=0))
        uv = [jnp.concatenate([ub[g], vb[:, gl[g]]], axis=0) for g in groups]
        bkw = [jnp.concatenate([bw[:, gl[g]], kw[:, gl[g]]], axis=0) for g in groups]
        full = [lax.dot_general(uv[g], bkw[g], _TN, preferred_element_type=F32) for g in groups]
        for g in groups:
            upd = sum(jnp.where(lane_head == h, full[g][h * N:(h + 1) * N, :], 0.0) for h in range(GH))
            s_scr[g] = s_old[g] * w_end[:, gl[g]] + upd
        return carry

    lax.fori_loop(0, n_chunks, chunk, 0)

    @pl.when(j == pl.num_programs(1) - 1)
    def _():
        for g in groups:
            s = s_scr[g]
            for h in range(GH):
                sfin_ref[0, g * GH + h] = s[:, h * N:(h + 1) * N]


def _slab(t):
    return t if isinstance(t, tuple) else (t.reshape((1,) + t.shape), 0)


def _wkv_seq(r, k, v, lw, a, b, s0, seq_len):
    srcs = [_slab(t) for t in (r, k, v, lw, a, b)]
    _, M, DI = srcs[0][0].shape
    B = M // seq_len
    H = DI // A_HEAD
    tb = min(seq_len, 256)
    nt = seq_len // tb
    blk = pl.BlockSpec((tb, DI), lambda bi, j: (bi * nt + j, 0))
    slab_blk = lambda p: pl.BlockSpec((1, tb, DI), lambda bi, j: (p, bi * nt + j, 0))
    sblk = pl.BlockSpec((1, H, A_HEAD, A_HEAD), lambda bi, j: (bi, 0, 0, 0))
    r, k, v, lw, a, b = (t for t, _ in srcs)
    return pl.pallas_call(
        _wkv_body,
        grid=(B, nt),
        in_specs=[slab_blk(p) for _, p in srcs] + [sblk],
        out_specs=[blk, sblk],
        out_shape=[jax.ShapeDtypeStruct((M, DI), F32), jax.ShapeDtypeStruct(s0.shape, F32)],
        scratch_shapes=[pltpu.VMEM((H // WKV_LANE_HEADS, A_HEAD, WKV_LANE_HEADS * A_HEAD), F32)],
        compiler_params=_params("arbitrary", "arbitrary"),
        name="rwkv_wkv_chunked",
    )(r, k, v, lw, a, b, s0)


def _wkv_tok_body(s_ref, r_ref, lw_ref, k_ref, vcol_ref, a_ref, b_ref, snew_ref, y_ref):
    s = s_ref[...]
    bc = lambda ref: ref[...][:, :, None, :]
    sa = jnp.sum(s * bc(a_ref), axis=-1, keepdims=True)
    s_new = s * jnp.exp(bc(lw_ref)) + sa * bc(b_ref) + vcol_ref[...] * bc(k_ref)
    snew_ref[...] = s_new
    y_ref[...] = jnp.sum(s_new * bc(r_ref), axis=-1, keepdims=True)


def _wkv_tok(s0, r, lw, k, v, a, b):
    B, H, N, _ = s0.shape
    bb = min(B, 2)
    heads = lambda t: t.reshape(B, H, N)
    vcol = jnp.broadcast_to(heads(v)[..., None], (B, H, N, N))
    sspec = pl.BlockSpec((bb, H, N, N), lambda i: (i, 0, 0, 0))
    vspec = pl.BlockSpec((bb, H, N), lambda i: (i, 0, 0))
    s_new, y = pl.pallas_call(
        _wkv_tok_body,
        grid=(B // bb,),
        in_specs=[sspec, vspec, vspec, vspec, sspec, vspec, vspec],
        out_specs=[sspec, pl.BlockSpec((bb, H, N, 1), lambda i: (i, 0, 0, 0))],
        out_shape=[jax.ShapeDtypeStruct(s0.shape, F32), jax.ShapeDtypeStruct((B, H, N, 1), F32)],
        compiler_params=_params("arbitrary"),
        name="rwkv_wkv_step",
    )(s0, heads(r), heads(lw), heads(k), vcol, heads(a), heads(b))
    return y.reshape(B, H * N), s_new


def _rwkv_out_body(y_ref, r_ref, kp_ref, vp_ref, z_ref, x_ref, lw_ref, lb_ref, rk_ref,
                   ones_ref, wo_ref, o_ref):
    n = float(A_HEAD)
    ones_bd = ones_ref[...]
    y = y_ref[...]
    yc = y - _head_sums(y, ones_bd, True) / n
    var = _head_sums(yc * yc, ones_bd, False) / n
    yn = yc * lax.rsqrt(var + GN_EPS) * lw_ref[...] + lb_ref[...]
    vp = vp_ref[0] if len(vp_ref.shape) == 3 else vp_ref[...]
    bonus = _head_sums(r_ref[0] * kp_ref[...] * rk_ref[...], ones_bd, False) * vp
    z = z_ref[0]
    g = (yn + bonus) * (z * _sigmoid(z))
    o_ref[...] = x_ref[...] + jnp.dot(g.astype(BF16), wo_ref[...], preferred_element_type=F32)


def _rwkv_out(y, rkvz, kp, vp, x, lnx_w, lnx_b, r_k, ones_bd, w_out):
    M, DI = y.shape
    D = x.shape[1]
    tm = min(M, 256)
    row = lambda i: (i, 0)
    full = lambda i: (0, 0)
    rkvz_spec = lambda p: pl.BlockSpec((1, tm, DI), lambda i: (p, i, 0))
    if vp is None:
        vp_in, vp_spec = rkvz, rkvz_spec(2)
    else:
        vp_in, vp_spec = vp, pl.BlockSpec((tm, DI), row)
    vec = pl.BlockSpec((1, DI), full)
    return pl.pallas_call(
        _rwkv_out_body,
        grid=(M // tm,),
        in_specs=[pl.BlockSpec((tm, DI), row), rkvz_spec(0), pl.BlockSpec((tm, DI), row), vp_spec,
                  rkvz_spec(3), pl.BlockSpec((tm, D), row), vec, vec, vec,
                  pl.BlockSpec(ones_bd.shape, full), pl.BlockSpec(w_out.shape, full)],
        out_specs=pl.BlockSpec((tm, D), row),
        out_shape=jax.ShapeDtypeStruct((M, D), F32),
        compiler_params=_params("arbitrary"),
        name="rwkv_out_proj",
    )(y, rkvz, kp, vp_in, rkvz, x, lnx_w, lnx_b, r_k, ones_bd, w_out)


def _mla_kv_body(c_lat, r_dim, x_ref, nw_ref, w_ref, kvn_ref, cos_ref, sin_ref, ckv_ref, kpe_ref):
    kv = _dot(_rms(x_ref[...], nw_ref[...]), w_ref[...])
    ckv_ref[...] = _rms(kv[:, :c_lat], kvn_ref[...])
    pe = kv[:, c_lat:c_lat + r_dim]
    pe_rot = kv[:, c_lat + 128:c_lat + 128 + r_dim]
    kpe_ref[...] = pe * cos_ref[...] + pe_rot * sin_ref[...]


def _mla_kv(x, norm_w, w_ext, kv_norm_w, cos, sin, c_lat, r_dim, seq_len):
    M, D = x.shape
    tm = min(seq_len if seq_len > 1 else M, 256)
    pos_tiles = max(seq_len // tm, 1)
    row = lambda i: (i, 0)
    full = lambda i: (0, 0)
    pos = lambda i: (i % pos_tiles, 0)
    return pl.pallas_call(
        functools.partial(_mla_kv_body, c_lat, r_dim),
        grid=(M // tm,),
        in_specs=[pl.BlockSpec((tm, D), row), pl.BlockSpec((1, D), full), pl.BlockSpec(w_ext.shape, full),
                  pl.BlockSpec((1, c_lat), full), pl.BlockSpec((tm, r_dim), pos), pl.BlockSpec((tm, r_dim), pos)],
        out_specs=[pl.BlockSpec((tm, c_lat), row), pl.BlockSpec((tm, r_dim), row)],
        out_shape=[jax.ShapeDtypeStruct((M, c_lat), F32), jax.ShapeDtypeStruct((M, r_dim), F32)],
        compiler_params=_params("arbitrary"),
        name="mla_shared_kv",
    )(x, norm_w, w_ext, kv_norm_w, cos, sin)


def _mla_expand_body(n_heads, d_nope, d_v, ckv_ref, kpe_ref, w_ref, k_ref, v_ref):
    kv = _dot(ckv_ref[...], w_ref[...]).astype(BF16)
    pe = kpe_ref[...].astype(BF16)
    v0 = n_heads * d_nope
    for h in range(n_heads):
        k_ref[h, :, :d_nope] = kv[:, h * d_nope:(h + 1) * d_nope]
        k_ref[h, :, d_nope:] = pe
        v_ref[h] = kv[:, v0 + h * d_v:v0 + (h + 1) * d_v]


def _mla_expand(ckv, kpe, w_ukv, n_heads, d_nope, d_v):
    M, c_lat = ckv.shape
    r_dim = kpe.shape[1]
    tm = min(M, 256)
    row = lambda i: (i, 0)
    return pl.pallas_call(
        functools.partial(_mla_expand_body, n_heads, d_nope, d_v),
        grid=(M // tm,),
        in_specs=[pl.BlockSpec((tm, c_lat), row), pl.BlockSpec((tm, r_dim), row),
                  pl.BlockSpec(w_ukv.shape, lambda i: (0, 0))],
        out_specs=[pl.BlockSpec((n_heads, tm, d_nope + r_dim), lambda i: (0, i, 0)),
                   pl.BlockSpec((n_heads, tm, d_v), lambda i: (0, i, 0))],
        out_shape=[jax.ShapeDtypeStruct((n_heads, M, d_nope + r_dim), BF16),
                   jax.ShapeDtypeStruct((n_heads, M, d_v), BF16)],
        compiler_params=_params("arbitrary"),
        name="mla_expand_kv",
    )(ckv, kpe, w_ukv)


def _mla_q_body(n_heads, d_nope, r_dim, q_lat, scale,
                x_ref, nw_ref, win_ref, qn_ref, wuq_ref, cos_ref, sin_ref, q_ref, z_ref):
    proj = _dot(_rms(x_ref[...], nw_ref[...]), win_ref[...])
    z_ref[...] = proj[:, q_lat:]
    q = _dot(_rms(proj[:, :q_lat], qn_ref[...]), wuq_ref[...]) * scale
    n0 = n_heads * d_nope
    n1 = n0 + n_heads * r_dim
    pe = q[:, n0:n1] * cos_ref[...] + q[:, n1:] * sin_ref[...]
    for h in range(n_heads):
        q_ref[h, :, :d_nope] = q[:, h * d_nope:(h + 1) * d_nope].astype(BF16)
        q_ref[h, :, d_nope:] = pe[:, h * r_dim:(h + 1) * r_dim].astype(BF16)


def _mla_q(x, norm_w, w_in, q_norm_w, w_uq_ext, cos_q, sin_q, n_heads, d_nope, r_dim, scale, seq_len):
    M, D = x.shape
    q_lat = q_norm_w.shape[1]
    d_gate = w_in.shape[1] - q_lat
    tm = min(seq_len if seq_len > 1 else M, 256)
    pos_tiles = max(seq_len // tm, 1)
    row = lambda i: (i, 0)
    full = lambda i: (0, 0)
    pos = lambda i: (i % pos_tiles, 0)
    return pl.pallas_call(
        functools.partial(_mla_q_body, n_heads, d_nope, r_dim, q_lat, scale),
        grid=(M // tm,),
        in_specs=[pl.BlockSpec((tm, D), row), pl.BlockSpec((1, D), full), pl.BlockSpec(w_in.shape, full),
                  pl.BlockSpec((1, q_lat), full), pl.BlockSpec(w_uq_ext.shape, full),
                  pl.BlockSpec((tm, n_heads * r_dim), pos), pl.BlockSpec((tm, n_heads * r_dim), pos)],
        out_specs=[pl.BlockSpec((n_heads, tm, d_nope + r_dim), lambda i: (0, i, 0)),
                   pl.BlockSpec((tm, d_gate), row)],
        out_shape=[jax.ShapeDtypeStruct((n_heads, M, d_nope + r_dim), BF16),
                   jax.ShapeDtypeStruct((M, d_gate), F32)],
        compiler_params=_params("arbitrary"),
        name="mla_query",
    )(x, norm_w, w_in, q_norm_w, w_uq_ext, cos_q, sin_q)


def _flash_body(t, q_ref, k_ref, v_ref, o_ref):
    blocks = range(k_ref.shape[1] // t)
    causal = (lax.broadcasted_iota(jnp.int32, (t, t), 1) <= lax.broadcasted_iota(jnp.int32, (t, t), 0))
    blk = lambda ref, n: ref[0, n * t:(n + 1) * t, :]
    q = [blk(q_ref, n) for n in blocks]
    s_diag = [jnp.where(causal, lax.dot_general(q[n], blk(k_ref, n), _NT, preferred_element_type=F32), -jnp.inf)
              for n in blocks]
    s_past = [None] + [lax.dot_general(q[n], k_ref[0, :n * t, :], _NT, preferred_element_type=F32)
                       for n in blocks[1:]]
    m = [jnp.max(s_diag[n], axis=-1, keepdims=True) for n in blocks]
    m = [m[0]] + [jnp.maximum(m[n], jnp.max(s_past[n], axis=-1, keepdims=True)) for n in blocks[1:]]
    p_diag = [jnp.exp(s_diag[n] - m[n]) for n in blocks]
    p_past = [None] + [jnp.exp(s_past[n] - m[n]) for n in blocks[1:]]
    l = [jnp.sum(p_diag[n], axis=-1, keepdims=True) for n in blocks]
    l = [l[0]] + [l[n] + jnp.sum(p_past[n], axis=-1, keepdims=True) for n in blocks[1:]]
    o = [jnp.dot(p_diag[n].astype(BF16), blk(v_ref, n), preferred_element_type=F32) for n in blocks]
    o = [o[0]] + [o[n] + jnp.dot(p_past[n].astype(BF16), v_ref[0, :n * t, :], preferred_element_type=F32)
                  for n in blocks[1:]]
    for n in blocks:
        o_ref[n * t:(n + 1) * t, :] = o[n] / l[n]


def _flash(q, k, v, seq_len):
    H, M, dk = q.shape
    dv = v.shape[2]
    B = M // seq_len
    t = min(seq_len, FLASH_BLOCK)
    seq = lambda d: pl.BlockSpec((1, seq_len, d), lambda b, h: (h, b, 0))
    return pl.pallas_call(
        functools.partial(_flash_body, t),
        grid=(B, H),
        in_specs=[seq(dk), seq(dk), seq(dv)],
        out_specs=pl.BlockSpec((seq_len, dv), lambda b, h: (b, h)),
        out_shape=jax.ShapeDtypeStruct((M, H * dv), F32),
        compiler_params=_params("arbitrary", "arbitrary"),
        name="mla_prompt_attention",
    )(q, k, v)


def _mla_out_body(with_norm, o_ref, z_ref, x_ref, wo_ref, fw_ref, xo_ref, *maybe_y):
    z = z_ref[...]
    g = o_ref[...] * (z * _sigmoid(z))
    xn = x_ref[...] + jnp.dot(g.astype(BF16), wo_ref[...], preferred_element_type=F32)
    xo_ref[...] = xn
    if with_norm:
        maybe_y[0][...] = _rms(xn, fw_ref[...])


def _mla_out(o, z, x, w_out, final_w, with_norm):
    M, DI = o.shape
    D = x.shape[1]
    tm = min(M, 256)
    row = lambda i: (i, 0)
    full = lambda i: (0, 0)
    n_out = 2 if with_norm else 1
    outs = pl.pallas_call(
        functools.partial(_mla_out_body, with_norm),
        grid=(M // tm,),
        in_specs=[pl.BlockSpec((tm, DI), row), pl.BlockSpec((tm, DI), row), pl.BlockSpec((tm, D), row),
                  pl.BlockSpec(w_out.shape, full), pl.BlockSpec((1, D), full)],
        out_specs=[pl.BlockSpec((tm, D), row)] * n_out,
        out_shape=[jax.ShapeDtypeStruct((M, D), F32)] * n_out,
        compiler_params=_params("arbitrary"),
        name="mla_out_proj",
    )(o, z, x, w_out, final_w)
    return outs if with_norm else (outs[0], None)


def _headwise_mm_body(x_ref, w_ref, o_ref):
    o_ref[0] = jnp.dot(x_ref[0].astype(BF16), w_ref[0], preferred_element_type=F32).astype(o_ref.dtype)


def _headwise_mm(x, w, out_dtype, name):
    H, B, K = x.shape
    N = w.shape[2]
    return pl.pallas_call(
        _headwise_mm_body,
        grid=(H,),
        in_specs=[pl.BlockSpec((1, B, K), lambda h: (h, 0, 0)), pl.BlockSpec((1, K, N), lambda h: (h, 0, 0))],
        out_specs=pl.BlockSpec((1, B, N), lambda h: (h, 0, 0)),
        out_shape=jax.ShapeDtypeStruct((H, B, N), out_dtype),
        compiler_params=_params("arbitrary"),
        name=name,
    )(x, w)


DECODE_PAGES_PER_STEP = 32


def _decode_body(n_pg, pt_ref, ql_ref, qp_ref, cn_ref, pn_ref, *refs):
    ckv_refs = refs[:n_pg]
    kpe_refs = refs[n_pg:2 * n_pg]
    o_ref, m_scr, l_scr, acc_scr = refs[2 * n_pg:]
    j = pl.program_id(1)
    ql = ql_ref[0]
    qp = qp_ref[0]

    @pl.when(j == 0)
    def _():
        cn = cn_ref[0]
        s_new = (jnp.sum(ql.astype(F32) * cn, axis=-1, keepdims=True)
                 + jnp.sum(qp.astype(F32) * pn_ref[0], axis=-1, keepdims=True))
        m_scr[...] = jnp.broadcast_to(s_new, m_scr.shape)
        l_scr[...] = jnp.ones(l_scr.shape, F32)
        acc_scr[...] = jnp.broadcast_to(cn, acc_scr.shape)

    ckv = jnp.concatenate([r[0].astype(BF16) for r in ckv_refs], axis=0)
    kpe_t = jnp.concatenate([r[0].astype(BF16) for r in kpe_refs], axis=1)
    s = (lax.dot_general(ql, ckv, _NT, preferred_element_type=F32)
         + jnp.dot(qp, kpe_t, preferred_element_type=F32))
    m_old = m_scr[...]
    m_new = jnp.maximum(m_old, jnp.max(s, axis=-1, keepdims=True))
    alpha = jnp.exp(m_old - m_new)
    p = jnp.exp(s - m_new[:, :1])
    l_new = alpha * l_scr[...] + jnp.sum(p, axis=-1, keepdims=True)
    acc = alpha[:, :1] * acc_scr[...] + jnp.dot(p.astype(BF16), ckv, preferred_element_type=F32)
    m_scr[...] = m_new
    l_scr[...] = l_new
    acc_scr[...] = acc

    @pl.when(j == pl.num_programs(1) - 1)
    def _():
        o_ref[0] = acc / l_new[:, :1]


def _decode_attn(q_lat, q_pe, ckv_new, kpe_new, cache_ckv, cache_kpe, page_table):
    B, H, C = q_lat.shape
    R = q_pe.shape[2]
    page = cache_ckv.shape[1]
    n_pages = page_table.shape[1]
    n_pg = min(DECODE_PAGES_PER_STEP, n_pages)
    ckv_spec = lambda i: pl.BlockSpec((1, page, C), lambda b, j, pt: (pt[b, j * n_pg + i], 0, 0))
    kpe_spec = lambda i: pl.BlockSpec((1, R, page), lambda b, j, pt: (pt[b, j * n_pg + i], 0, 0))
    per_b = lambda b, j, pt: (b, 0, 0)
    grid_spec = pltpu.PrefetchScalarGridSpec(
        num_scalar_prefetch=1,
        grid=(B, n_pages // n_pg),
        in_specs=[pl.BlockSpec((1, H, C), per_b), pl.BlockSpec((1, H, R), per_b),
                  pl.BlockSpec((1, 1, C), per_b), pl.BlockSpec((1, 1, R), per_b)]
                 + [ckv_spec(i) for i in range(n_pg)] + [kpe_spec(i) for i in range(n_pg)],
        out_specs=pl.BlockSpec((1, H, C), per_b),
        scratch_shapes=[pltpu.VMEM((H, 128), F32), pltpu.VMEM((H, 128), F32), pltpu.VMEM((H, C), F32)],
    )
    return pl.pallas_call(
        functools.partial(_decode_body, n_pg),
        grid_spec=grid_spec,
        out_shape=jax.ShapeDtypeStruct((B, H, C), F32),
        compiler_params=_params("arbitrary", "arbitrary"),
        name="mla_sample_attention",
    )(page_table, q_lat, q_pe, ckv_new, kpe_new, *([cache_ckv] * n_pg), *([cache_kpe] * n_pg))


def _rot_half_cols(w, r_dim):
    lead = w.shape[:-1]
    wb = w.reshape(lead + (-1, 2, r_dim // 2))
    return jnp.stack([-wb[..., 1, :], wb[..., 0, :]], axis=-2).reshape(w.shape)


def _rope_tables(pos, r_dim):
    half = r_dim // 2
    inv_freq = ROPE_THETA ** (-jnp.arange(half, dtype=F32) / half)
    ang = pos.astype(F32)[:, None] * inv_freq[None, :]
    cos, sin = jnp.cos(ang), jnp.sin(ang)
    return jnp.concatenate([cos, cos], axis=1), jnp.concatenate([sin, sin], axis=1)


def kernel(x_prompt, x_sample, state_wkv, state_shift, cache_ckv, cache_kpe, page_table, a_norm_w, a_mix, a_w_in, a_w0, a_w1, a_w2, a_a0, a_a1, a_a2, a_v0, a_v1, a_v2, a_k_k, a_k_a, a_r_k, a_lnx_w, a_lnx_b, a_w_out, kv_in_norm_w, w_dkv, kv_norm_w, w_uk, w_uv, b_norm_w, b_w_in, b_q_norm_w, b_w_uq, b_w_out, final_norm_w):
    Bp, Sp, D = x_prompt.shape
    Bs, Ts, _ = x_sample.shape
    assert Ts == 1
    n_a = a_norm_w.shape[0]
    n_b = b_norm_w.shape[0]
    DI = a_w_in.shape[3]
    H = DI // A_HEAD
    c_lat, n_bh, d_nope = w_uk.shape
    d_v = w_uv.shape[2]
    r_dim = cache_kpe.shape[2]
    past = page_table.shape[1] * cache_ckv.shape[1]
    scale = float(d_nope + r_dim) ** -0.5
    bf = lambda t: t.astype(BF16)
    vec = lambda t: t.reshape(1, -1)

    head_of_lane = jnp.arange(DI, dtype=jnp.int32) // A_HEAD
    ones_bd = (head_of_lane[:HEAD_SUM_WIDTH, None] == head_of_lane[None, :HEAD_SUM_WIDTH]).astype(BF16)

    xp = x_prompt.reshape(Bp * Sp, D)
    xs = x_sample.reshape(Bs, D)
    shift0 = jnp.zeros((Bp, D), F32)
    wkv0 = jnp.zeros((Bp, H, A_HEAD, A_HEAD), F32)
    vf_p = vf_s = None
    wkv_p, sh_p, wkv_s, sh_s = [], [], [], []

    for l in range(n_a):
        w_in = bf(a_w_in[l])
        lw_w = (bf(a_w1[l]), bf(a_w2[l]), vec(a_w0[l]))
        lora_a = (bf(a_a1[l]), bf(a_a2[l]), vec(a_a0[l]))
        lora_v = None if l == 0 else (bf(a_v1[l - 1]), bf(a_v2[l - 1]), vec(a_v0[l - 1]))
        k_k, k_a = vec(a_k_k[l]), vec(a_k_a[l])
        lnx_w, lnx_b, r_k = vec(a_lnx_w[l]), vec(a_lnx_b[l]), vec(a_r_k[l])
        w_out = bf(a_w_out[l])

        def layer(x, prev, s0, v_first, seq_len):
            mixed, hlast = _premix(x, prev, a_norm_w[l], a_mix[l], seq_len)
            rkvz = _bmm(mixed, w_in, F32, 4)
            outs = _lora(mixed, rkvz, v_first, lw_w, lora_a, lora_v, k_k, k_a, ones_bd)
            lw, am, bm, kp = outs[:4]
            vp = outs[4] if lora_v is not None else None
            if seq_len == 1:
                y, s_fin = _wkv_tok(s0, rkvz[0], lw, kp, rkvz[2] if vp is None else vp, am, bm)
            else:
                y, s_fin = _wkv_seq((rkvz, 0), kp, (rkvz, 2) if vp is None else vp, lw, am, bm, s0, seq_len)
            x_new = _rwkv_out(y, rkvz, kp, vp, x, lnx_w, lnx_b, r_k, ones_bd, w_out)
            return x_new, (rkvz, 2), s_fin, hlast

        xp, v, S, last = layer(xp, shift0, wkv0, vf_p, Sp)
        wkv_p.append(S)
        sh_p.append(last)
        if l == 0:
            vf_p = v
        xs, v, S, last = layer(xs, state_shift[l], state_wkv[l], vf_s, 1)
        wkv_s.append(S)
        sh_s.append(last)
        if l == 0:
            vf_s = v

    pad = jnp.zeros((D, 128 - r_dim), F32)
    w_pe = w_dkv[:, c_lat:]
    w_dkv_ext = bf(jnp.concatenate([w_dkv[:, :c_lat], w_pe, pad, _rot_half_cols(w_pe, r_dim), pad], axis=1))
    cos_p, sin_p = _rope_tables(jnp.arange(Sp, dtype=jnp.int32), r_dim)
    cos_s, sin_s = _rope_tables(jnp.full((Bs,), past, dtype=jnp.int32), r_dim)
    ckv_p, kpe_p = _mla_kv(xp, vec(kv_in_norm_w), w_dkv_ext, vec(kv_norm_w), cos_p, sin_p, c_lat, r_dim, Sp)
    ckv_s, kpe_s = _mla_kv(xs, vec(kv_in_norm_w), w_dkv_ext, vec(kv_norm_w), cos_s, sin_s, c_lat, r_dim, 1)
    w_ukv = bf(jnp.concatenate([w_uk.reshape(c_lat, n_bh * d_nope), w_uv.reshape(c_lat, n_bh * d_v)], axis=1))
    k_cat, v_heads = _mla_expand(ckv_p, kpe_p, w_ukv, n_bh, d_nope, d_v)
    w_uk_t = bf(jnp.transpose(w_uk, (1, 2, 0)))
    w_uv_h = bf(jnp.transpose(w_uv, (1, 0, 2)))
    cache_kpe_t = jnp.swapaxes(cache_kpe, 1, 2)
    tile_h = lambda t: jnp.tile(t, (1, n_bh))
    cos_pq, sin_pq, cos_sq, sin_sq = tile_h(cos_p), tile_h(sin_p), tile_h(cos_s), tile_h(sin_s)

    y_p = y_s = None
    for l in range(n_b):
        last_layer = l == n_b - 1
        w_in = bf(b_w_in[l])
        n0 = n_bh * d_nope
        uq = b_w_uq[l].reshape(-1, n_bh, d_nope + r_dim)
        uq_pe = uq[:, :, d_nope:].reshape(-1, n_bh * r_dim)
        w_uq_ext = bf(jnp.concatenate([uq[:, :, :d_nope].reshape(-1, n0), uq_pe, _rot_half_cols(uq_pe, r_dim)], axis=1))
        w_out = bf(b_w_out[l])
        nw, qnw, fw = vec(b_norm_w[l]), vec(b_q_norm_w[l]), vec(final_norm_w)

        q_cat, z = _mla_q(xp, nw, w_in, qnw, w_uq_ext, cos_pq, sin_pq, n_bh, d_nope, r_dim, scale, Sp)
        o = _flash(q_cat, k_cat, v_heads, Sp)
        xp, y_p = _mla_out(o, z, xp, w_out, fw, last_layer)

        q_cat, z = _mla_q(xs, nw, w_in, qnw, w_uq_ext, cos_sq, sin_sq, n_bh, d_nope, r_dim, scale, 1)
        q_lat = _headwise_mm(q_cat[:, :, :d_nope], w_uk_t, BF16, "mla_absorb_q")
        o_lat = _decode_attn(jnp.transpose(q_lat, (1, 0, 2)), jnp.transpose(q_cat[:, :, d_nope:], (1, 0, 2)),
                             ckv_s.reshape(Bs, 1, c_lat), kpe_s.reshape(Bs, 1, r_dim),
                             cache_ckv, cache_kpe_t, page_table)
        o_h = _headwise_mm(jnp.transpose(o_lat, (1, 0, 2)), w_uv_h, F32, "mla_value_up")
        o = jnp.transpose(o_h, (1, 0, 2)).reshape(Bs, n_bh * d_v)
        xs, y_s = _mla_out(o, z, xs, w_out, fw, last_layer)

    return (y_p.reshape(Bp, Sp, D), y_s.reshape(Bs, Ts, D),
            jnp.stack(wkv_p), jnp.stack(sh_p), ckv_p.reshape(Bp, Sp, c_lat), kpe_p.reshape(Bp, Sp, r_dim),
            jnp.stack(wkv_s), jnp.stack(sh_s), ckv_s.reshape(Bs, Ts, c_lat), kpe_s.reshape(Bs, Ts, r_dim))
```

```python
import functools
import math

import jax
import jax.numpy as jnp
from jax import lax
from jax.experimental import pallas as pl
from jax.experimental.pallas import tpu as pltpu

F32 = jnp.float32
BF16 = jnp.bfloat16

NORM_EPS = 1e-6
ROPE_THETA = 10000.0
A_HEAD = 64
GN_EPS = A_HEAD * 1e-5
WKV_CHUNK = 64
WKV_LANE_HEADS = 4
HEAD_SUM_WIDTH = 256
FLASH_BLOCK = 512
V7X_VMEM_LIMIT = 48 * 1024 * 1024

_NT = (((1,), (1,)), ((), ()))
_TN = (((0,), (0,)), ((), ()))


def _params(*sem):
    return pltpu.CompilerParams(dimension_semantics=sem, vmem_limit_bytes=V7X_VMEM_LIMIT)


def _dot(a, b):
    return jnp.dot(a.astype(BF16), b.astype(BF16), preferred_element_type=F32)


def _split2(x):
    hi = x.astype(BF16)
    lo = (x - hi.astype(F32)).astype(BF16)
    return hi, lo


def _split3(x):
    hi = x.astype(BF16)
    r = x - hi.astype(F32)
    mid = r.astype(BF16)
    lo = (r - mid.astype(F32)).astype(BF16)
    return hi, mid, lo


def _head_sums(t, ones_bd, two_pass):
    w = ones_bd.shape[0]
    cols = []
    for i in range(t.shape[1] // w):
        blk = t[:, i * w:(i + 1) * w]
        if two_pass:
            hi, lo = _split2(blk)
            cols.append(jnp.dot(hi, ones_bd, preferred_element_type=F32)
                        + jnp.dot(lo, ones_bd, preferred_element_type=F32))
        else:
            cols.append(jnp.dot(blk.astype(BF16), ones_bd, preferred_element_type=F32))
    return jnp.concatenate(cols, axis=1)


def _sigmoid(x):
    return 1.0 / (1.0 + jnp.exp(-x))


def _rms(x, w):
    return x * lax.rsqrt(jnp.mean(x * x, axis=-1, keepdims=True) + NORM_EPS) * w


def _premix_seq_body(seq_tiles, x_ref, prev_ref, nw_ref, mix_ref, mixed_ref, hlast_ref, carry_ref):
    i = pl.program_id(0)
    h = _rms(x_ref[...], nw_ref[...])
    tm = h.shape[0]
    prev_row = jnp.where(i % seq_tiles == 0, prev_ref[0], carry_ref[...])
    row = lax.broadcasted_iota(jnp.int32, h.shape, 0)
    shifted = jnp.where(row == 0, prev_row, pltpu.roll(h, 1, axis=0))
    last = h[tm - 1:tm, :]
    carry_ref[...] = last
    hlast_ref[0] = last
    dx = shifted - h
    for p in range(6):
        mixed_ref[p] = (h + dx * mix_ref[p:p + 1, :]).astype(BF16)


def _premix_tok_body(x_ref, prev_ref, nw_ref, mix_ref, mixed_ref, hlast_ref):
    h = _rms(x_ref[...], nw_ref[...])
    hlast_ref[...] = h
    dx = prev_ref[...] - h
    for p in range(6):
        mixed_ref[p] = (h + dx * mix_ref[p:p + 1, :]).astype(BF16)


def _premix(x, prev, norm_w, mix, seq_len):
    M, D = x.shape
    B = M // seq_len
    nw = norm_w.reshape(1, D)
    out_mixed = jax.ShapeDtypeStruct((6, M, D), BF16)
    if seq_len == 1:
        tm = min(M, 256)
        return pl.pallas_call(
            _premix_tok_body,
            grid=(M // tm,),
            in_specs=[pl.BlockSpec((tm, D), lambda i: (i, 0)),
                      pl.BlockSpec((tm, D), lambda i: (i, 0)),
                      pl.BlockSpec((1, D), lambda i: (0, 0)),
                      pl.BlockSpec((6, D), lambda i: (0, 0))],
            out_specs=[pl.BlockSpec((6, tm, D), lambda i: (0, i, 0)),
                       pl.BlockSpec((tm, D), lambda i: (i, 0))],
            out_shape=[out_mixed, jax.ShapeDtypeStruct((B, D), F32)],
            compiler_params=_params("arbitrary"),
            name="rwkv_premix_tok",
        )(x, prev, nw, mix)
    tm = min(seq_len, 256)
    seq_tiles = seq_len // tm
    mixed, hlast = pl.pallas_call(
        functools.partial(_premix_seq_body, seq_tiles),
        grid=(M // tm,),
        in_specs=[pl.BlockSpec((tm, D), lambda i: (i, 0)),
                  pl.BlockSpec((1, 1, D), lambda i: (i // seq_tiles, 0, 0)),
                  pl.BlockSpec((1, D), lambda i: (0, 0)),
                  pl.BlockSpec((6, D), lambda i: (0, 0))],
        out_specs=[pl.BlockSpec((6, tm, D), lambda i: (0, i, 0)),
                   pl.BlockSpec((1, 1, D), lambda i: (i // seq_tiles, 0, 0))],
        out_shape=[out_mixed, jax.ShapeDtypeStruct((B, 1, D), F32)],
        scratch_shapes=[pltpu.VMEM((1, D), F32)],
        compiler_params=_params("arbitrary"),
        name="rwkv_premix_seq",
    )(x, prev.reshape(B, 1, D), nw, mix)
    return mixed, hlast.reshape(B, D)


def _bmm_body(x_ref, w_ref, o_ref):
    o_ref[0] = jnp.dot(x_ref[0], w_ref[0], preferred_element_type=F32).astype(o_ref.dtype)


def _bmm(x, w, out_dtype, n_batch, tm=512, tn=2048):
    _, M, K = x.shape
    P, _, N = w.shape
    assert P == n_batch
    tm, tn = min(tm, M), min(tn, N)
    return pl.pallas_call(
        _bmm_body,
        grid=(P, N // tn, M // tm),
        in_specs=[pl.BlockSpec((1, tm, K), lambda p, n, m: (p, m, 0)),
                  pl.BlockSpec((1, K, tn), lambda p, n, m: (p, 0, n))],
        out_specs=pl.BlockSpec((1, tm, tn), lambda p, n, m: (p, m, n)),
        out_shape=jax.ShapeDtypeStruct((P, M, N), out_dtype),
        compiler_params=_params("arbitrary", "arbitrary", "arbitrary"),
        name="rwkv_in_proj",
    )(x, w)


def _lora_body(has_vres, *refs):
    if has_vres:
        (xw_ref, xa_ref, xv_ref, k_ref, v_ref, vf_ref,
         w1_ref, w2_ref, w0_ref, a1_ref, a2_ref, a0_ref, v1_ref, v2_ref, v0_ref,
         kk_ref, ones_ref,
         lw_ref, kkn_ref, alr_ref, vp_ref) = refs
    else:
        (xw_ref, xa_ref, k_ref,
         w1_ref, w2_ref, w0_ref, a1_ref, a2_ref, a0_ref,
         kk_ref, ones_ref,
         lw_ref, kkn_ref, alr_ref) = refs
    xd = w0_ref[...] + _dot(jnp.tanh(_dot(xw_ref[0], w1_ref[...])), w2_ref[...])
    lw_ref[...] = -math.exp(-0.5) * _sigmoid(xd)
    alr_ref[...] = _sigmoid(a0_ref[...] + _dot(_dot(xa_ref[0], a1_ref[...]), a2_ref[...])).astype(BF16)
    kk = k_ref[0].astype(F32) * kk_ref[...]
    kkn_ref[...] = (kk / jnp.maximum(jnp.sqrt(_head_sums(kk * kk, ones_ref[...], True)), 1e-12)).astype(BF16)
    if has_vres:
        v = v_ref[0].astype(F32)
        vg = _sigmoid(v0_ref[...] + _dot(_dot(xv_ref[0], v1_ref[...]), v2_ref[...]))
        vp_ref[...] = (v + (vf_ref[0].astype(F32) - v) * vg).astype(BF16)


def _lora(mixed, rkvz, v_first, lw_w, lora_a, lora_v, k_k, ones_bd):
    _, M, D = mixed.shape
    DI = rkvz.shape[2]
    tm = min(M, 256)
    has_vres = lora_v is not None
    row = lambda i: (i, 0)
    full = lambda i: (0, 0)
    mixed_spec = lambda p: pl.BlockSpec((1, tm, D), lambda i: (p, i, 0))
    rkvz_spec = lambda p: pl.BlockSpec((1, tm, DI), lambda i: (p, i, 0))
    wspec = lambda a: pl.BlockSpec(a.shape, full)
    ins = [mixed, mixed]
    specs = [mixed_spec(4), mixed_spec(5)]
    if has_vres:
        ins += [mixed]
        specs += [mixed_spec(2)]
    ins += [rkvz]
    specs += [rkvz_spec(1)]
    if has_vres:
        vf, vf_slab = _slab(v_first)
        ins += [rkvz, vf]
        specs += [rkvz_spec(2), pl.BlockSpec((1, tm, DI), lambda i: (vf_slab, i, 0))]
    weights = list(lw_w) + list(lora_a) + (list(lora_v) if has_vres else []) + [k_k, ones_bd]
    ins += weights
    specs += [wspec(a) for a in weights]
    out_dtypes = [F32, BF16, BF16] + ([BF16] if has_vres else [])
    return pl.pallas_call(
        functools.partial(_lora_body, has_vres),
        grid=(M // tm,),
        in_specs=specs,
        out_specs=[pl.BlockSpec((tm, DI), row)] * len(out_dtypes),
        out_shape=[jax.ShapeDtypeStruct((M, DI), dt) for dt in out_dtypes],
        compiler_params=_params("arbitrary"),
        name="rwkv_lora",
    )(*ins)


def _wkv_body(r_ref, k_ref, v_ref, lw_ref, kkn_ref, alr_ref, ka_ref, s0_ref, y_ref, sfin_ref, s_scr):
    j = pl.program_id(1)
    C = WKV_CHUNK
    N = A_HEAD
    GH = WKV_LANE_HEADS
    GL = GH * N
    assert C == N
    shift = N.bit_length() - 1
    groups = range(s_scr.shape[0])
    n_chunks = r_ref.shape[1] // C

    @pl.when(j == 0)
    def _():
        for g in groups:
            s_scr[g] = jnp.concatenate([s0_ref[0, g * GH + h] for h in range(GH)], axis=1)

    row = lax.broadcasted_iota(jnp.int32, (C, GL), 0)
    lane = lax.broadcasted_iota(jnp.int32, (C, GL), 1)
    lane_in_head = jnp.bitwise_and(lane, N - 1)
    lane_head = jnp.right_shift(lane, shift)
    strict = row > lane_in_head
    incl = (row >= lane_in_head).astype(F32)
    incl2 = jnp.concatenate([incl, incl], axis=1) > 0.5
    eye = (row == lane_in_head).astype(F32)
    tri = (lax.broadcasted_iota(jnp.int32, (C, C), 0) >= lax.broadcasted_iota(jnp.int32, (C, C), 1)).astype(BF16)
    blockdiag = (jnp.right_shift(lax.broadcasted_iota(jnp.int32, (GL, GL), 0), shift)
                 == jnp.right_shift(lax.broadcasted_iota(jnp.int32, (GL, GL), 1), shift)).astype(BF16)

    def bd(x):
        return jnp.concatenate([x.astype(BF16)] * GH, axis=0) * blockdiag

    def dotf(a, b):
        return jnp.dot(a, b, preferred_element_type=F32)

    def chunk(c, carry):
        rows = pl.ds(pl.multiple_of(c * C, C), C)
        lw = lw_ref[0, rows, :]
        l_hi, l_lo = _split2(lw)
        cs = dotf(tri, l_hi) + dotf(tri, l_lo)
        cs_last = cs[C - 1:C, :]
        kkn, alr, k, r, v = (ref[0, rows, :].astype(F32) for ref in (kkn_ref, alr_ref, k_ref, r_ref, v_ref))
        a = -kkn
        b = kkn * alr
        k = k * (1.0 + (alr - 1.0) * ka_ref[...])
        e_neg = jnp.exp(-cs)
        e_rem = jnp.exp(cs_last - cs)
        at = (a * jnp.exp(cs - lw)).astype(BF16)
        rt = (r * jnp.exp(cs)).astype(BF16)
        bt = (b * e_neg).astype(BF16)
        kt = (k * e_neg).astype(BF16)
        bw = (b * e_rem).astype(BF16)
        kw = (k * e_rem).astype(BF16)
        vb = v.astype(BF16)
        w_end = jnp.exp(cs_last)
        gl = [slice(g * GL, (g + 1) * GL) for g in groups]
        ar = [jnp.concatenate([at[:, gl[g]], rt[:, gl[g]]], axis=0) for g in groups]
        bk = [jnp.concatenate([bd(bt[:, gl[g]]), bd(kt[:, gl[g]])], axis=0) for g in groups]
        p = [lax.dot_general(ar[g], bk[g], _NT, preferred_element_type=F32) for g in groups]
        low = [jnp.where(strict, p[g][:C, :GL], 0.0) for g in groups]
        m_ak = [jnp.where(strict, p[g][:C, GL:], 0.0).astype(BF16) for g in groups]
        m_r = [jnp.where(incl2, p[g][C:, :], 0.0).astype(BF16) for g in groups]
        s_old = [s_scr[g] for g in groups]
        ay0 = [lax.dot_general(ar[g], bd(s_old[g]), _NT, preferred_element_type=F32) for g in groups]
        bdv = [bd(vb[:, gl[g]]) for g in groups]
        r0 = [ay0[g][:C] + dotf(m_ak[g], bdv[g]) for g in groups]
        x = [dotf(low[g].astype(BF16), bd(low[g])) for g in groups]
        acc = [eye + low[g] for g in groups]
        span = 2
        while 2 * span < C:
            xa = [dotf(jnp.concatenate([x[g], acc[g]], axis=0).astype(BF16), bd(x[g])) for g in groups]
            x = [xa[g][:C] for g in groups]
            acc = [acc[g] + xa[g][C:] for g in groups]
            span *= 2
        inv = [acc[g] + dotf(acc[g].astype(BF16), bd(x[g])) for g in groups]
        ub = [dotf(inv[g].astype(BF16), bd(r0[g])).astype(BF16) for g in groups]
        for g in groups:
            y_ref[rows, gl[g]] = ay0[g][C:] + dotf(m_r[g], jnp.concatenate([bd(ub[g]), bdv[g]], axis=0))
        uv = [jnp.concatenate([ub[g], vb[:, gl[g]]], axis=0) for g in groups]
        bkw = [jnp.concatenate([bw[:, gl[g]], kw[:, gl[g]]], axis=0) for g in groups]
        full = [lax.dot_general(uv[g], bkw[g], _TN, preferred_element_type=F32) for g in groups]
        for g in groups:
            upd = sum(jnp.where(lane_head == h, full[g][h * N:(h + 1) * N, :], 0.0) for h in range(GH))
            s_scr[g] = s_old[g] * w_end[:, gl[g]] + upd
        return carry

    lax.fori_loop(0, n_chunks, chunk, 0)

    @pl.when(j == pl.num_programs(1) - 1)
    def _():
        for g in groups:
            s = s_scr[g]
            for h in range(GH):
                sfin_ref[0, g * GH + h] = s[:, h * N:(h + 1) * N]


def _slab(t):
    return t if isinstance(t, tuple) else (t.reshape((1,) + t.shape), 0)


def _wkv_seq(r, k, v, lw, kkn, alr, k_a, s0, seq_len):
    srcs = [_slab(t) for t in (r, k, v, lw, kkn, alr)]
    _, M, DI = srcs[0][0].shape
    B = M // seq_len
    H = DI // A_HEAD
    tb = min(seq_len, 256)
    nt = seq_len // tb
    blk = pl.BlockSpec((tb, DI), lambda bi, j: (bi * nt + j, 0))
    slab_blk = lambda p: pl.BlockSpec((1, tb, DI), lambda bi, j: (p, bi * nt + j, 0))
    sblk = pl.BlockSpec((1, H, A_HEAD, A_HEAD), lambda bi, j: (bi, 0, 0, 0))
    return pl.pallas_call(
        _wkv_body,
        grid=(B, nt),
        in_specs=[slab_blk(p) for _, p in srcs] + [pl.BlockSpec((1, DI), lambda bi, j: (0, 0)), sblk],
        out_specs=[blk, sblk],
        out_shape=[jax.ShapeDtypeStruct((M, DI), F32), jax.ShapeDtypeStruct(s0.shape, F32)],
        scratch_shapes=[pltpu.VMEM((H // WKV_LANE_HEADS, A_HEAD, WKV_LANE_HEADS * A_HEAD), F32)],
        compiler_params=_params("arbitrary", "arbitrary"),
        name="rwkv_wkv_chunked",
    )(*(t for t, _ in srcs), k_a, s0)


WKV_TOK_HEADS = 2


def _wkv_tok_body(layer, s_ref, r_ref, lw_ref, k_ref, v_ref, kkn_ref, alr_ref, ka_ref, *rest):
    snew_ref, y_ref = rest[-2:]
    if snew_ref.shape[0] > 1:
        for other in range(snew_ref.shape[0]):
            if other != layer:
                snew_ref[other] = jnp.zeros(snew_ref.shape[1:], F32)
        snew_ref = snew_ref.at[layer:layer + 1]
    for i in range(s_ref.shape[1]):
        s = s_ref[0, i]
        kkn, alr = kkn_ref[i], alr_ref[i]
        k = k_ref[i] * (1.0 + (alr - 1.0) * ka_ref[i])
        sa = jnp.sum(s * (-kkn)[None], axis=1, keepdims=True)
        s_new = s * jnp.exp(lw_ref[i])[None] + sa * (kkn * alr)[None] + v_ref[i][:, None, :] * k[None]
        snew_ref[0, i] = s_new
        y_ref[i] = jnp.sum(s_new * r_ref[i][None], axis=1)


def _wkv_tok(layer, state_t, prev_out, r, lw, k, v, kkn, alr, k_a):
    L, H, N, _, B = state_t.shape
    hb = WKV_TOK_HEADS
    heads = lambda t: t.astype(F32).T.reshape(H, N, B)
    ka = jnp.broadcast_to(k_a.reshape(H, N, 1), (H, N, B))
    sspec = pl.BlockSpec((1, hb, N, N, B), lambda h: (layer, h, 0, 0, 0))
    vspec = pl.BlockSpec((hb, N, B), lambda h: (h, 0, 0))
    ins = [state_t] + [heads(t) for t in (r, lw, k, v, kkn, alr)] + [ka]
    specs = [sspec] + [vspec] * 7
    aliases = {}
    out_sspec = sspec
    if prev_out is None:
        out_sspec = pl.BlockSpec((L, hb, N, N, B), lambda h: (0, h, 0, 0, 0))
    else:
        aliases = {len(ins): 0}
        ins.append(prev_out)
        specs.append(pl.BlockSpec(memory_space=pl.ANY))
    s_out, y = pl.pallas_call(
        functools.partial(_wkv_tok_body, layer),
        grid=(H // hb,),
        in_specs=specs,
        out_specs=[out_sspec, vspec],
        out_shape=[jax.ShapeDtypeStruct(state_t.shape, F32), jax.ShapeDtypeStruct((H, N, B), F32)],
        input_output_aliases=aliases,
        compiler_params=_params("arbitrary"),
        name="rwkv_wkv_step",
    )(*ins)
    return y.reshape(H * N, B).T, s_out


def _rwkv_out_body(y_ref, r_ref, k_ref, v_ref, z_ref, alr_ref, x_ref, lw_ref, lb_ref, rk_ref, ka_ref,
                   ones_ref, wo_ref, o_ref):
    n = float(A_HEAD)
    ones_bd = ones_ref[...]
    y = y_ref[...]
    yc = y - _head_sums(y, ones_bd, True) / n
    var = _head_sums(yc * yc, ones_bd, False) / n
    yn = yc * lax.rsqrt(var + GN_EPS) * lw_ref[...] + lb_ref[...]
    r, k, v, z = (ref[0].astype(F32) for ref in (r_ref, k_ref, v_ref, z_ref))
    k = k * (1.0 + (alr_ref[...].astype(F32) - 1.0) * ka_ref[...])
    bonus = _head_sums(r * k * rk_ref[...], ones_bd, False) * v
    g = (yn + bonus) * (z * _sigmoid(z))
    o_ref[...] = x_ref[...] + jnp.dot(g.astype(BF16), wo_ref[...], preferred_element_type=F32)


def _rwkv_out(y, rkvz, v, alr, x, lnx_w, lnx_b, r_k, k_a, ones_bd, w_out):
    M, DI = y.shape
    D = x.shape[1]
    tm = min(M, 256)
    row = lambda i: (i, 0)
    full = lambda i: (0, 0)
    slab_spec = lambda p: pl.BlockSpec((1, tm, DI), lambda i: (p, i, 0))
    v_in, v_slab = _slab(v)
    vec = pl.BlockSpec((1, DI), full)
    return pl.pallas_call(
        _rwkv_out_body,
        grid=(M // tm,),
        in_specs=[pl.BlockSpec((tm, DI), row), slab_spec(0), slab_spec(1), slab_spec(v_slab), slab_spec(3),
                  pl.BlockSpec((tm, DI), row), pl.BlockSpec((tm, D), row), vec, vec, vec, vec,
                  pl.BlockSpec(ones_bd.shape, full), pl.BlockSpec(w_out.shape, full)],
        out_specs=pl.BlockSpec((tm, D), row),
        out_shape=jax.ShapeDtypeStruct((M, D), F32),
        compiler_params=_params("arbitrary"),
        name="rwkv_out_proj",
    )(y, rkvz, rkvz, v_in, rkvz, alr, x, lnx_w, lnx_b, r_k, k_a, ones_bd, w_out)


def _mla_kv_body(c_lat, r_dim, x_ref, nw_ref, w_ref, kvn_ref, cos_ref, sin_ref, ckv_ref, kpe_ref):
    kv = _dot(_rms(x_ref[...], nw_ref[...]), w_ref[...])
    ckv_ref[...] = _rms(kv[:, :c_lat], kvn_ref[...])
    pe = kv[:, c_lat:c_lat + r_dim]
    pe_rot = kv[:, c_lat + 128:c_lat + 128 + r_dim]
    kpe_ref[...] = pe * cos_ref[...] + pe_rot * sin_ref[...]


def _mla_kv(x, norm_w, w_ext, kv_norm_w, cos, sin, c_lat, r_dim, seq_len):
    M, D = x.shape
    tm = min(seq_len if seq_len > 1 else M, 256)
    pos_tiles = max(seq_len // tm, 1)
    row = lambda i: (i, 0)
    full = lambda i: (0, 0)
    pos = lambda i: (i % pos_tiles, 0)
    return pl.pallas_call(
        functools.partial(_mla_kv_body, c_lat, r_dim),
        grid=(M // tm,),
        in_specs=[pl.BlockSpec((tm, D), row), pl.BlockSpec((1, D), full), pl.BlockSpec(w_ext.shape, full),
                  pl.BlockSpec((1, c_lat), full), pl.BlockSpec((tm, r_dim), pos), pl.BlockSpec((tm, r_dim), pos)],
        out_specs=[pl.BlockSpec((tm, c_lat), row), pl.BlockSpec((tm, r_dim), row)],
        out_shape=[jax.ShapeDtypeStruct((M, c_lat), F32), jax.ShapeDtypeStruct((M, r_dim), F32)],
        compiler_params=_params("arbitrary"),
        name="mla_shared_kv",
    )(x, norm_w, w_ext, kv_norm_w, cos, sin)


def _mla_expand_body(n_heads, d_nope, d_v, ckv_ref, kpe_ref, w_ref, k_ref, v_ref):
    kv = _dot(ckv_ref[...], w_ref[...]).astype(BF16)
    pe = kpe_ref[...].astype(BF16)
    v0 = n_heads * d_nope
    for h in range(n_heads):
        k_ref[h, :, :d_nope] = kv[:, h * d_nope:(h + 1) * d_nope]
        k_ref[h, :, d_nope:] = pe
        v_ref[h] = kv[:, v0 + h * d_v:v0 + (h + 1) * d_v]


def _mla_expand(ckv, kpe, w_ukv, n_heads, d_nope, d_v):
    M, c_lat = ckv.shape
    r_dim = kpe.shape[1]
    tm = min(M, 256)
    row = lambda i: (i, 0)
    return pl.pallas_call(
        functools.partial(_mla_expand_body, n_heads, d_nope, d_v),
        grid=(M // tm,),
        in_specs=[pl.BlockSpec((tm, c_lat), row), pl.BlockSpec((tm, r_dim), row),
                  pl.BlockSpec(w_ukv.shape, lambda i: (0, 0))],
        out_specs=[pl.BlockSpec((n_heads, tm, d_nope + r_dim), lambda i: (0, i, 0)),
                   pl.BlockSpec((n_heads, tm, d_v), lambda i: (0, i, 0))],
        out_shape=[jax.ShapeDtypeStruct((n_heads, M, d_nope + r_dim), BF16),
                   jax.ShapeDtypeStruct((n_heads, M, d_v), BF16)],
        compiler_params=_params("arbitrary"),
        name="mla_expand_kv",
    )(ckv, kpe, w_ukv)


def _mla_q_body(n_heads, d_nope, r_dim, q_lat, scale,
                x_ref, nw_ref, win_ref, qn_ref, wuq_ref, cos_ref, sin_ref, q_ref, z_ref):
    proj = _dot(_rms(x_ref[...], nw_ref[...]), win_ref[...])
    z_ref[...] = proj[:, q_lat:].astype(z_ref.dtype)
    q = _dot(_rms(proj[:, :q_lat], qn_ref[...]), wuq_ref[...]) * scale
    n0 = n_heads * d_nope
    n1 = n0 + n_heads * r_dim
    pe = q[:, n0:n1] * cos_ref[...] + q[:, n1:] * sin_ref[...]
    for h in range(n_heads):
        q_ref[h, :, :d_nope] = q[:, h * d_nope:(h + 1) * d_nope].astype(BF16)
        q_ref[h, :, d_nope:] = pe[:, h * r_dim:(h + 1) * r_dim].astype(BF16)


def _mla_q(x, norm_w, w_in, q_norm_w, w_uq_ext, cos_q, sin_q, n_heads, d_nope, r_dim, scale, seq_len):
    M, D = x.shape
    q_lat = q_norm_w.shape[1]
    d_gate = w_in.shape[1] - q_lat
    tm = min(seq_len if seq_len > 1 else M, 256)
    pos_tiles = max(seq_len // tm, 1)
    row = lambda i: (i, 0)
    full = lambda i: (0, 0)
    pos = lambda i: (i % pos_tiles, 0)
    return pl.pallas_call(
        functools.partial(_mla_q_body, n_heads, d_nope, r_dim, q_lat, scale),
        grid=(M // tm,),
        in_specs=[pl.BlockSpec((tm, D), row), pl.BlockSpec((1, D), full), pl.BlockSpec(w_in.shape, full),
                  pl.BlockSpec((1, q_lat), full), pl.BlockSpec(w_uq_ext.shape, full),
                  pl.BlockSpec((tm, n_heads * r_dim), pos), pl.BlockSpec((tm, n_heads * r_dim), pos)],
        out_specs=[pl.BlockSpec((n_heads, tm, d_nope + r_dim), lambda i: (0, i, 0)),
                   pl.BlockSpec((tm, d_gate), row)],
        out_shape=[jax.ShapeDtypeStruct((n_heads, M, d_nope + r_dim), BF16),
                   jax.ShapeDtypeStruct((M, d_gate), BF16)],
        compiler_params=_params("arbitrary"),
        name="mla_query",
    )(x, norm_w, w_in, q_norm_w, w_uq_ext, cos_q, sin_q)


def _flash_body(t, q_ref, k_ref, v_ref, o_ref):
    blocks = range(k_ref.shape[1] // t)
    causal = (lax.broadcasted_iota(jnp.int32, (t, t), 1) <= lax.broadcasted_iota(jnp.int32, (t, t), 0))
    blk = lambda ref, n: ref[0, n * t:(n + 1) * t, :]
    q = [blk(q_ref, n) for n in blocks]
    s_diag = [jnp.where(causal, lax.dot_general(q[n], blk(k_ref, n), _NT, preferred_element_type=F32), -jnp.inf)
              for n in blocks]
    s_past = [None] + [lax.dot_general(q[n], k_ref[0, :n * t, :], _NT, preferred_element_type=F32)
                       for n in blocks[1:]]
    m = [jnp.max(s_diag[n], axis=-1, keepdims=True) for n in blocks]
    m = [m[0]] + [jnp.maximum(m[n], jnp.max(s_past[n], axis=-1, keepdims=True)) for n in blocks[1:]]
    p_diag = [jnp.exp(s_diag[n] - m[n]) for n in blocks]
    p_past = [None] + [jnp.exp(s_past[n] - m[n]) for n in blocks[1:]]
    l = [jnp.sum(p_diag[n], axis=-1, keepdims=True) for n in blocks]
    l = [l[0]] + [l[n] + jnp.sum(p_past[n], axis=-1, keepdims=True) for n in blocks[1:]]
    o = [jnp.dot(p_diag[n].astype(BF16), blk(v_ref, n), preferred_element_type=F32) for n in blocks]
    o = [o[0]] + [o[n] + jnp.dot(p_past[n].astype(BF16), v_ref[0, :n * t, :], preferred_element_type=F32)
                  for n in blocks[1:]]
    for n in blocks:
        o_ref[n * t:(n + 1) * t, :] = (o[n] / l[n]).astype(o_ref.dtype)


def _flash(q, k, v, seq_len):
    H, M, dk = q.shape
    dv = v.shape[2]
    B = M // seq_len
    t = min(seq_len, FLASH_BLOCK)
    seq = lambda d: pl.BlockSpec((1, seq_len, d), lambda b, h: (h, b, 0))
    return pl.pallas_call(
        functools.partial(_flash_body, t),
        grid=(B, H),
        in_specs=[seq(dk), seq(dk), seq(dv)],
        out_specs=pl.BlockSpec((seq_len, dv), lambda b, h: (b, h)),
        out_shape=jax.ShapeDtypeStruct((M, H * dv), BF16),
        compiler_params=_params("arbitrary", "arbitrary"),
        name="mla_prompt_attention",
    )(q, k, v)


def _mla_out_body(with_norm, o_ref, z_ref, x_ref, wo_ref, fw_ref, xo_ref, *maybe_y):
    z = z_ref[...].astype(F32)
    g = o_ref[...].astype(F32) * (z * _sigmoid(z))
    xn = x_ref[...] + jnp.dot(g.astype(BF16), wo_ref[...], preferred_element_type=F32)
    xo_ref[...] = xn
    if with_norm:
        maybe_y[0][...] = _rms(xn, fw_ref[...])


def _mla_out(o, z, x, w_out, final_w, with_norm):
    M, DI = o.shape
    D = x.shape[1]
    tm = min(M, 256)
    row = lambda i: (i, 0)
    full = lambda i: (0, 0)
    n_out = 2 if with_norm else 1
    outs = pl.pallas_call(
        functools.partial(_mla_out_body, with_norm),
        grid=(M // tm,),
        in_specs=[pl.BlockSpec((tm, DI), row), pl.BlockSpec((tm, DI), row), pl.BlockSpec((tm, D), row),
                  pl.BlockSpec(w_out.shape, full), pl.BlockSpec((1, D), full)],
        out_specs=[pl.BlockSpec((tm, D), row)] * n_out,
        out_shape=[jax.ShapeDtypeStruct((M, D), F32)] * n_out,
        compiler_params=_params("arbitrary"),
        name="mla_out_proj",
    )(o, z, x, w_out, final_w)
    return outs if with_norm else (outs[0], None)


def _headwise_mm_body(x_ref, w_ref, o_ref):
    o_ref[0] = jnp.dot(x_ref[0].astype(BF16), w_ref[0], preferred_element_type=F32).astype(o_ref.dtype)


def _headwise_mm(x, w, out_dtype, name):
    H, B, K = x.shape
    N = w.shape[2]
    return pl.pallas_call(
        _headwise_mm_body,
        grid=(H,),
        in_specs=[pl.BlockSpec((1, B, K), lambda h: (h, 0, 0)), pl.BlockSpec((1, K, N), lambda h: (h, 0, 0))],
        out_specs=pl.BlockSpec((1, B, N), lambda h: (h, 0, 0)),
        out_shape=jax.ShapeDtypeStruct((H, B, N), out_dtype),
        compiler_params=_params("arbitrary"),
        name=name,
    )(x, w)


DECODE_PAGES_PER_STEP = 32


def _decode_body(n_pg, pt_ref, ql_ref, qp_ref, cn_ref, pn_ref, *refs):
    ckv_refs = refs[:n_pg]
    kpe_refs = refs[n_pg:2 * n_pg]
    o_ref, m_scr, l_scr, acc_scr = refs[2 * n_pg:]
    j = pl.program_id(1)
    ql = ql_ref[0]
    qp = qp_ref[0]

    @pl.when(j == 0)
    def _():
        cn = cn_ref[0]
        s_new = (jnp.sum(ql.astype(F32) * cn, axis=-1, keepdims=True)
                 + jnp.sum(qp.astype(F32) * pn_ref[0], axis=-1, keepdims=True))
        m_scr[...] = jnp.broadcast_to(s_new, m_scr.shape)
        l_scr[...] = jnp.ones(l_scr.shape, F32)
        acc_scr[...] = jnp.broadcast_to(cn, acc_scr.shape)

    ckv = jnp.concatenate([r[0].astype(BF16) for r in ckv_refs], axis=0)
    kpe_t = jnp.concatenate([r[0].astype(BF16) for r in kpe_refs], axis=1)
    s = (lax.dot_general(ql, ckv, _NT, preferred_element_type=F32)
         + jnp.dot(qp, kpe_t, preferred_element_type=F32))
    m_old = m_scr[...]
    m_new = jnp.maximum(m_old, jnp.max(s, axis=-1, keepdims=True))
    alpha = jnp.exp(m_old - m_new)
    p = jnp.exp(s - m_new[:, :1])
    l_new = alpha * l_scr[...] + jnp.sum(p, axis=-1, keepdims=True)
    acc = alpha[:, :1] * acc_scr[...] + jnp.dot(p.astype(BF16), ckv, preferred_element_type=F32)
    m_scr[...] = m_new
    l_scr[...] = l_new
    acc_scr[...] = acc

    @pl.when(j == pl.num_programs(1) - 1)
    def _():
        o_ref[0] = acc / l_new[:, :1]


def _decode_attn(q_lat, q_pe, ckv_new, kpe_new, cache_ckv, cache_kpe, page_table):
    B, H, C = q_lat.shape
    R = q_pe.shape[2]
    page = cache_ckv.shape[1]
    n_pages = page_table.shape[1]
    n_pg = min(DECODE_PAGES_PER_STEP, n_pages)
    ckv_spec = lambda i: pl.BlockSpec((1, page, C), lambda b, j, pt: (pt[b, j * n_pg + i], 0, 0))
    kpe_spec = lambda i: pl.BlockSpec((1, R, page), lambda b, j, pt: (pt[b, j * n_pg + i], 0, 0))
    per_b = lambda b, j, pt: (b, 0, 0)
    grid_spec = pltpu.PrefetchScalarGridSpec(
        num_scalar_prefetch=1,
        grid=(B, n_pages // n_pg),
        in_specs=[pl.BlockSpec((1, H, C), per_b), pl.BlockSpec((1, H, R), per_b),
                  pl.BlockSpec((1, 1, C), per_b), pl.BlockSpec((1, 1, R), per_b)]
                 + [ckv_spec(i) for i in range(n_pg)] + [kpe_spec(i) for i in range(n_pg)],
        out_specs=pl.BlockSpec((1, H, C), per_b),
        scratch_shapes=[pltpu.VMEM((H, 128), F32), pltpu.VMEM((H, 128), F32), pltpu.VMEM((H, C), F32)],
    )
    return pl.pallas_call(
        functools.partial(_decode_body, n_pg),
        grid_spec=grid_spec,
        out_shape=jax.ShapeDtypeStruct((B, H, C), F32),
        compiler_params=_params("arbitrary", "arbitrary"),
        name="mla_sample_attention",
    )(page_table, q_lat, q_pe, ckv_new, kpe_new, *([cache_ckv] * n_pg), *([cache_kpe] * n_pg))


def _rot_half_cols(w, r_dim):
    lead = w.shape[:-1]
    wb = w.reshape(lead + (-1, 2, r_dim // 2))
    return jnp.stack([-wb[..., 1, :], wb[..., 0, :]], axis=-2).reshape(w.shape)


def _rope_tables(pos, r_dim):
    half = r_dim // 2
    inv_freq = ROPE_THETA ** (-jnp.arange(half, dtype=F32) / half)
    ang = pos.astype(F32)[:, None] * inv_freq[None, :]
    cos, sin = jnp.cos(ang), jnp.sin(ang)
    return jnp.concatenate([cos, cos], axis=1), jnp.concatenate([sin, sin], axis=1)


def kernel(x_prompt, x_sample, state_wkv, state_shift, cache_ckv, cache_kpe, page_table, a_norm_w, a_mix, a_w_in, a_w0, a_w1, a_w2, a_a0, a_a1, a_a2, a_v0, a_v1, a_v2, a_k_k, a_k_a, a_r_k, a_lnx_w, a_lnx_b, a_w_out, kv_in_norm_w, w_dkv, kv_norm_w, w_uk, w_uv, b_norm_w, b_w_in, b_q_norm_w, b_w_uq, b_w_out, final_norm_w):
    Bp, Sp, D = x_prompt.shape
    Bs, Ts, _ = x_sample.shape
    assert Ts == 1
    n_a = a_norm_w.shape[0]
    n_b = b_norm_w.shape[0]
    DI = a_w_in.shape[3]
    H = DI // A_HEAD
    c_lat, n_bh, d_nope = w_uk.shape
    d_v = w_uv.shape[2]
    r_dim = cache_kpe.shape[2]
    past = page_table.shape[1] * cache_ckv.shape[1]
    scale = float(d_nope + r_dim) ** -0.5
    bf = lambda t: t.astype(BF16)
    vec = lambda t: t.reshape(1, -1)

    head_of_lane = jnp.arange(DI, dtype=jnp.int32) // A_HEAD
    ones_bd = (head_of_lane[:HEAD_SUM_WIDTH, None] == head_of_lane[None, :HEAD_SUM_WIDTH]).astype(BF16)

    xp = x_prompt.reshape(Bp * Sp, D)
    xs = x_sample.reshape(Bs, D)
    shift0 = jnp.zeros((Bp, D), F32)
    wkv0 = jnp.zeros((Bp, H, A_HEAD, A_HEAD), F32)
    vf_p = vf_s = None
    wkv_p, sh_p, sh_s = [], [], []
    state_t = jnp.transpose(state_wkv, (0, 2, 3, 4, 1))
    wkv_s_t = None

    for l in range(n_a):
        w_in = bf(a_w_in[l])
        lw_w = (bf(a_w1[l]), bf(a_w2[l]), vec(a_w0[l]))
        lora_a = (bf(a_a1[l]), bf(a_a2[l]), vec(a_a0[l]))
        lora_v = None if l == 0 else (bf(a_v1[l - 1]), bf(a_v2[l - 1]), vec(a_v0[l - 1]))
        k_k, k_a = vec(a_k_k[l]), vec(a_k_a[l])
        lnx_w, lnx_b, r_k = vec(a_lnx_w[l]), vec(a_lnx_b[l]), vec(a_r_k[l])
        w_out = bf(a_w_out[l])

        def layer(x, prev, s0, v_first, seq_len):
            mixed, hlast = _premix(x, prev, a_norm_w[l], a_mix[l], seq_len)
            rkvz = _bmm(mixed, w_in, BF16, 4)
            outs = _lora(mixed, rkvz, v_first, lw_w, lora_a, lora_v, k_k, ones_bd)
            lw, kkn, alr = outs[:3]
            v = (rkvz, 2) if lora_v is None else outs[3]
            if seq_len == 1:
                v2d = rkvz[2] if lora_v is None else v
                y, s_fin = _wkv_tok(l, state_t, s0, rkvz[0], lw, rkvz[1], v2d, kkn, alr, k_a)
            else:
                y, s_fin = _wkv_seq((rkvz, 0), (rkvz, 1), v, lw, kkn, alr, k_a, s0, seq_len)
            x_new = _rwkv_out(y, rkvz, v, alr, x, lnx_w, lnx_b, r_k, k_a, ones_bd, w_out)
            return x_new, (rkvz, 2), s_fin, hlast

        xp, v, S, last = layer(xp, shift0, wkv0, vf_p, Sp)
        wkv_p.append(S)
        sh_p.append(last)
        if l == 0:
            vf_p = v
        xs, v, wkv_s_t, last = layer(xs, state_shift[l], wkv_s_t, vf_s, 1)
        sh_s.append(last)
        if l == 0:
            vf_s = v

    pad = jnp.zeros((D, 128 - r_dim), F32)
    w_pe = w_dkv[:, c_lat:]
    w_dkv_ext = bf(jnp.concatenate([w_dkv[:, :c_lat], w_pe, pad, _rot_half_cols(w_pe, r_dim), pad], axis=1))
    cos_p, sin_p = _rope_tables(jnp.arange(Sp, dtype=jnp.int32), r_dim)
    cos_s, sin_s = _rope_tables(jnp.full((Bs,), past, dtype=jnp.int32), r_dim)
    ckv_p, kpe_p = _mla_kv(xp, vec(kv_in_norm_w), w_dkv_ext, vec(kv_norm_w), cos_p, sin_p, c_lat, r_dim, Sp)
    ckv_s, kpe_s = _mla_kv(xs, vec(kv_in_norm_w), w_dkv_ext, vec(kv_norm_w), cos_s, sin_s, c_lat, r_dim, 1)
    w_ukv = bf(jnp.concatenate([w_uk.reshape(c_lat, n_bh * d_nope), w_uv.reshape(c_lat, n_bh * d_v)], axis=1))
    k_cat, v_heads = _mla_expand(ckv_p, kpe_p, w_ukv, n_bh, d_nope, d_v)
    w_uk_t = bf(jnp.transpose(w_uk, (1, 2, 0)))
    w_uv_h = bf(jnp.transpose(w_uv, (1, 0, 2)))
    cache_kpe_t = jnp.swapaxes(cache_kpe, 1, 2)
    tile_h = lambda t: jnp.tile(t, (1, n_bh))
    cos_pq, sin_pq, cos_sq, sin_sq = tile_h(cos_p), tile_h(sin_p), tile_h(cos_s), tile_h(sin_s)

    y_p = y_s = None
    for l in range(n_b):
        last_layer = l == n_b - 1
        w_in = bf(b_w_in[l])
        n0 = n_bh * d_nope
        uq = b_w_uq[l].reshape(-1, n_bh, d_nope + r_dim)
        uq_pe = uq[:, :, d_nope:].reshape(-1, n_bh * r_dim)
        w_uq_ext = bf(jnp.concatenate([uq[:, :, :d_nope].reshape(-1, n0), uq_pe, _rot_half_cols(uq_pe, r_dim)], axis=1))
        w_out = bf(b_w_out[l])
        nw, qnw, fw = vec(b_norm_w[l]), vec(b_q_norm_w[l]), vec(final_norm_w)

        q_cat, z = _mla_q(xp, nw, w_in, qnw, w_uq_ext, cos_pq, sin_pq, n_bh, d_nope, r_dim, scale, Sp)
        o = _flash(q_cat, k_cat, v_heads, Sp)
        xp, y_p = _mla_out(o, z, xp, w_out, fw, last_layer)

        q_cat, z = _mla_q(xs, nw, w_in, qnw, w_uq_ext, cos_sq, sin_sq, n_bh, d_nope, r_dim, scale, 1)
        q_lat = _headwise_mm(q_cat[:, :, :d_nope], w_uk_t, BF16, "mla_absorb_q")
        o_lat = _decode_attn(jnp.transpose(q_lat, (1, 0, 2)), jnp.transpose(q_cat[:, :, d_nope:], (1, 0, 2)),
                             ckv_s.reshape(Bs, 1, c_lat), kpe_s.reshape(Bs, 1, r_dim),
                             cache_ckv, cache_kpe_t, page_table)
        o_h = _headwise_mm(jnp.transpose(o_lat, (1, 0, 2)), w_uv_h, F32, "mla_value_up")
        o = jnp.transpose(o_h, (1, 0, 2)).reshape(Bs, n_bh * d_v)
        xs, y_s = _mla_out(o, z, xs, w_out, fw, last_layer)

    return (y_p.reshape(Bp, Sp, D), y_s.reshape(Bs, Ts, D),
            jnp.stack(wkv_p), jnp.stack(sh_p), ckv_p.reshape(Bp, Sp, c_lat), kpe_p.reshape(Bp, Sp, r_dim),
            jnp.transpose(wkv_s_t, (0, 4, 1, 2, 3)), jnp.stack(sh_s),
            ckv_s.reshape(Bs, Ts, c_lat), kpe_s.reshape(Bs, Ts, r_dim))
```

```python
import functools
import math

import jax
import jax.numpy as jnp
from jax import lax
from jax.experimental import pallas as pl
from jax.experimental.pallas import tpu as pltpu

F32 = jnp.float32
BF16 = jnp.bfloat16

NORM_EPS = 1e-6
ROPE_THETA = 10000.0
A_HEAD = 64
GN_EPS = A_HEAD * 1e-5
WKV_CHUNK = 64
WKV_LANE_HEADS = 4
HEAD_SUM_WIDTH = 256
FLASH_BLOCK = 512
LANES = 128
ROW_TILE = 256
V7X_VMEM_LIMIT = 48 * 1024 * 1024

_NT = (((1,), (1,)), ((), ()))
_TN = (((0,), (0,)), ((), ()))


def _params(*sem):
    return pltpu.CompilerParams(dimension_semantics=sem, vmem_limit_bytes=V7X_VMEM_LIMIT)


def _dot(a, b):
    return jnp.dot(a.astype(BF16), b.astype(BF16), preferred_element_type=F32)


def _split2(x):
    hi = x.astype(BF16)
    lo = (x - hi.astype(F32)).astype(BF16)
    return hi, lo


def _head_sums(t, ones_bd, two_pass):
    w = ones_bd.shape[0]
    cols = []
    for i in range(t.shape[1] // w):
        blk = t[:, i * w:(i + 1) * w]
        if two_pass:
            hi, lo = _split2(blk)
            cols.append(jnp.dot(hi, ones_bd, preferred_element_type=F32)
                        + jnp.dot(lo, ones_bd, preferred_element_type=F32))
        else:
            cols.append(jnp.dot(blk.astype(BF16), ones_bd, preferred_element_type=F32))
    return jnp.concatenate(cols, axis=1)


def _sigmoid(x):
    return 1.0 / (1.0 + jnp.exp(-x))


def _rms(x, w):
    return x * lax.rsqrt(jnp.mean(x * x, axis=-1, keepdims=True) + NORM_EPS) * w


def _premix_seq_body(seq_tiles, x_ref, prev_ref, nw_ref, mix_ref, mixed_ref, hlast_ref, carry_ref):
    i = pl.program_id(0)
    h = _rms(x_ref[...], nw_ref[...])
    tm = h.shape[0]
    prev_row = jnp.where(i % seq_tiles == 0, prev_ref[0], carry_ref[...])
    row = lax.broadcasted_iota(jnp.int32, h.shape, 0)
    shifted = jnp.where(row == 0, prev_row, pltpu.roll(h, 1, axis=0))
    last = h[tm - 1:tm, :]
    carry_ref[...] = last
    hlast_ref[0] = last
    dx = shifted - h
    for p in range(6):
        mixed_ref[p] = (h + dx * mix_ref[p:p + 1, :]).astype(BF16)


def _premix_tok_body(x_ref, prev_ref, nw_ref, mix_ref, mixed_ref, hlast_ref):
    h = _rms(x_ref[...], nw_ref[...])
    hlast_ref[...] = h
    dx = prev_ref[...] - h
    for p in range(6):
        mixed_ref[p] = (h + dx * mix_ref[p:p + 1, :]).astype(BF16)


def _premix(x, prev, norm_w, mix, seq_len):
    M, D = x.shape
    B = M // seq_len
    nw = norm_w.reshape(1, D)
    out_mixed = jax.ShapeDtypeStruct((6, M, D), BF16)
    if seq_len == 1:
        tm = min(M, ROW_TILE)
        return pl.pallas_call(
            _premix_tok_body,
            grid=(M // tm,),
            in_specs=[pl.BlockSpec((tm, D), lambda i: (i, 0)),
                      pl.BlockSpec((tm, D), lambda i: (i, 0)),
                      pl.BlockSpec((1, D), lambda i: (0, 0)),
                      pl.BlockSpec((6, D), lambda i: (0, 0))],
            out_specs=[pl.BlockSpec((6, tm, D), lambda i: (0, i, 0)),
                       pl.BlockSpec((tm, D), lambda i: (i, 0))],
            out_shape=[out_mixed, jax.ShapeDtypeStruct((B, D), F32)],
            compiler_params=_params("arbitrary"),
            name="rwkv_premix_tok",
        )(x, prev, nw, mix)
    tm = min(seq_len, ROW_TILE)
    seq_tiles = seq_len // tm
    mixed, hlast = pl.pallas_call(
        functools.partial(_premix_seq_body, seq_tiles),
        grid=(M // tm,),
        in_specs=[pl.BlockSpec((tm, D), lambda i: (i, 0)),
                  pl.BlockSpec((1, 1, D), lambda i: (i // seq_tiles, 0, 0)),
                  pl.BlockSpec((1, D), lambda i: (0, 0)),
                  pl.BlockSpec((6, D), lambda i: (0, 0))],
        out_specs=[pl.BlockSpec((6, tm, D), lambda i: (0, i, 0)),
                   pl.BlockSpec((1, 1, D), lambda i: (i // seq_tiles, 0, 0))],
        out_shape=[out_mixed, jax.ShapeDtypeStruct((B, 1, D), F32)],
        scratch_shapes=[pltpu.VMEM((1, D), F32)],
        compiler_params=_params("arbitrary"),
        name="rwkv_premix_seq",
    )(x, prev.reshape(B, 1, D), nw, mix)
    return mixed, hlast.reshape(B, D)


def _bmm_body(x_ref, w_ref, o_ref):
    o_ref[0] = jnp.dot(x_ref[0], w_ref[0], preferred_element_type=F32).astype(o_ref.dtype)


def _bmm(x, w, out_dtype, n_batch, tm=512, tn=2048):
    _, M, K = x.shape
    P, _, N = w.shape
    assert P == n_batch
    tm, tn = min(tm, M), min(tn, N)
    return pl.pallas_call(
        _bmm_body,
        grid=(P, N // tn, M // tm),
        in_specs=[pl.BlockSpec((1, tm, K), lambda p, n, m: (p, m, 0)),
                  pl.BlockSpec((1, K, tn), lambda p, n, m: (p, 0, n))],
        out_specs=pl.BlockSpec((1, tm, tn), lambda p, n, m: (p, m, n)),
        out_shape=jax.ShapeDtypeStruct((P, M, N), out_dtype),
        compiler_params=_params("arbitrary", "arbitrary", "arbitrary"),
        name="rwkv_in_proj",
    )(x, w)


def _lora_body(has_vres, *refs):
    if has_vres:
        (xw_ref, xa_ref, xv_ref, k_ref, v_ref, vf_ref,
         w1_ref, w2_ref, w0_ref, a1_ref, a2_ref, a0_ref, v1_ref, v2_ref, v0_ref,
         kk_ref, ones_ref,
         lw_ref, kkn_ref, alr_ref, vp_ref) = refs
    else:
        (xw_ref, xa_ref, k_ref,
         w1_ref, w2_ref, w0_ref, a1_ref, a2_ref, a0_ref,
         kk_ref, ones_ref,
         lw_ref, kkn_ref, alr_ref) = refs
    xd = w0_ref[...] + _dot(jnp.tanh(_dot(xw_ref[0], w1_ref[...])), w2_ref[...])
    lw_ref[...] = -math.exp(-0.5) * _sigmoid(xd)
    alr_ref[...] = _sigmoid(a0_ref[...] + _dot(_dot(xa_ref[0], a1_ref[...]), a2_ref[...])).astype(BF16)
    kk = k_ref[0].astype(F32) * kk_ref[...]
    kkn_ref[...] = (kk / jnp.maximum(jnp.sqrt(_head_sums(kk * kk, ones_ref[...], True)), 1e-12)).astype(BF16)
    if has_vres:
        v = v_ref[0].astype(F32)
        vg = _sigmoid(v0_ref[...] + _dot(_dot(xv_ref[0], v1_ref[...]), v2_ref[...]))
        vp_ref[...] = (v + (vf_ref[0].astype(F32) - v) * vg).astype(BF16)


def _lora(mixed, rkvz, v_first, lw_w, lora_a, lora_v, k_k, ones_bd):
    _, M, D = mixed.shape
    DI = rkvz.shape[2]
    tm = min(M, ROW_TILE)
    has_vres = lora_v is not None
    row = lambda i: (i, 0)
    full = lambda i: (0, 0)
    mixed_spec = lambda p: pl.BlockSpec((1, tm, D), lambda i: (p, i, 0))
    rkvz_spec = lambda p: pl.BlockSpec((1, tm, DI), lambda i: (p, i, 0))
    wspec = lambda a: pl.BlockSpec(a.shape, full)
    ins = [mixed, mixed]
    specs = [mixed_spec(4), mixed_spec(5)]
    if has_vres:
        ins += [mixed]
        specs += [mixed_spec(2)]
    ins += [rkvz]
    specs += [rkvz_spec(1)]
    if has_vres:
        vf, vf_slab = _slab(v_first)
        ins += [rkvz, vf]
        specs += [rkvz_spec(2), pl.BlockSpec((1, tm, DI), lambda i: (vf_slab, i, 0))]
    weights = list(lw_w) + list(lora_a) + (list(lora_v) if has_vres else []) + [k_k, ones_bd]
    ins += weights
    specs += [wspec(a) for a in weights]
    out_dtypes = [F32, BF16, BF16] + ([BF16] if has_vres else [])
    return pl.pallas_call(
        functools.partial(_lora_body, has_vres),
        grid=(M // tm,),
        in_specs=specs,
        out_specs=[pl.BlockSpec((tm, DI), row)] * len(out_dtypes),
        out_shape=[jax.ShapeDtypeStruct((M, DI), dt) for dt in out_dtypes],
        compiler_params=_params("arbitrary"),
        name="rwkv_lora",
    )(*ins)


def _wkv_body(r_ref, k_ref, v_ref, lw_ref, kkn_ref, alr_ref, ka_ref, s0_ref, y_ref, sfin_ref, s_scr):
    j = pl.program_id(1)
    C = WKV_CHUNK
    N = A_HEAD
    GH = WKV_LANE_HEADS
    GL = GH * N
    assert C == N
    shift = N.bit_length() - 1
    groups = range(s_scr.shape[0])
    n_chunks = r_ref.shape[1] // C

    @pl.when(j == 0)
    def _():
        for g in groups:
            s_scr[g] = jnp.concatenate([s0_ref[0, g * GH + h] for h in range(GH)], axis=1)

    row = lax.broadcasted_iota(jnp.int32, (C, GL), 0)
    lane = lax.broadcasted_iota(jnp.int32, (C, GL), 1)
    lane_in_head = jnp.bitwise_and(lane, N - 1)
    lane_head = jnp.right_shift(lane, shift)
    strict = row > lane_in_head
    incl = row >= lane_in_head
    row2 = lax.broadcasted_iota(jnp.int32, (2 * C, GL), 0)
    lane2 = jnp.bitwise_and(lax.broadcasted_iota(jnp.int32, (2 * C, GL), 1), N - 1)
    strict_incl = row2 >= jnp.where(row2 < C, lane2 + 1, lane2 + C)
    eye = (row == lane_in_head).astype(F32)
    tri = (lax.broadcasted_iota(jnp.int32, (C, C), 0) >= lax.broadcasted_iota(jnp.int32, (C, C), 1)).astype(BF16)
    blockdiag = (jnp.right_shift(lax.broadcasted_iota(jnp.int32, (GL, GL), 0), shift)
                 == jnp.right_shift(lax.broadcasted_iota(jnp.int32, (GL, GL), 1), shift)).astype(BF16)

    def bd(x):
        return jnp.concatenate([x.astype(BF16)] * GH, axis=0) * blockdiag

    def dotf(a, b):
        return jnp.dot(a, b, preferred_element_type=F32)

    def chunk(c, carry):
        rows = pl.ds(pl.multiple_of(c * C, C), C)
        lw = lw_ref[0, rows, :]
        l_hi, l_lo = _split2(lw)
        cs = dotf(tri, l_hi) + dotf(tri, l_lo)
        cs_last = cs[C - 1:C, :]
        kkn, alr, k, r, v = (ref[0, rows, :].astype(F32) for ref in (kkn_ref, alr_ref, k_ref, r_ref, v_ref))
        a = -kkn
        b = kkn * alr
        k = k * (1.0 + (alr - 1.0) * ka_ref[...])
        e_neg = jnp.exp(-cs)
        e_rem = jnp.exp(cs_last - cs)
        at = (a * jnp.exp(cs - lw)).astype(BF16)
        rt = (r * jnp.exp(cs)).astype(BF16)
        bt = (b * e_neg).astype(BF16)
        kt = (k * e_neg).astype(BF16)
        bw = (b * e_rem).astype(BF16)
        kw = (k * e_rem).astype(BF16)
        vb = v.astype(BF16)
        w_end = jnp.exp(cs_last)
        gl = [slice(g * GL, (g + 1) * GL) for g in groups]
        ar = [jnp.concatenate([at[:, gl[g]], rt[:, gl[g]]], axis=0) for g in groups]
        bk = [jnp.concatenate([bd(bt[:, gl[g]]), bd(kt[:, gl[g]])], axis=0) for g in groups]
        p = [lax.dot_general(ar[g], bk[g], _NT, preferred_element_type=F32) for g in groups]
        low = [jnp.where(strict, p[g][:C, :GL], 0.0) for g in groups]
        m_k = [jnp.where(strict_incl, p[g][:, GL:], 0.0).astype(BF16) for g in groups]
        m_rb = [jnp.where(incl, p[g][C:, :GL], 0.0).astype(BF16) for g in groups]
        s_old = [s_scr[g] for g in groups]
        ay0 = [lax.dot_general(ar[g], bd(s_old[g]), _NT, preferred_element_type=F32) for g in groups]
        kv = [dotf(m_k[g], bd(vb[:, gl[g]])) for g in groups]
        r0 = [ay0[g][:C] + kv[g][:C] for g in groups]
        x = [dotf(low[g].astype(BF16), bd(low[g])) for g in groups]
        acc = [eye + low[g] for g in groups]
        span = 2
        while 2 * span < C:
            xa = [dotf(jnp.concatenate([x[g], acc[g]], axis=0).astype(BF16), bd(x[g])) for g in groups]
            x = [xa[g][:C] for g in groups]
            acc = [acc[g] + xa[g][C:] for g in groups]
            span *= 2
        inv = [acc[g] + dotf(acc[g].astype(BF16), bd(x[g])) for g in groups]
        ub = [dotf(inv[g].astype(BF16), bd(r0[g])).astype(BF16) for g in groups]
        for g in groups:
            y_ref[rows, gl[g]] = (ay0[g][C:] + kv[g][C:] + dotf(m_rb[g], bd(ub[g]))).astype(y_ref.dtype)
        uv = [jnp.concatenate([ub[g], vb[:, gl[g]]], axis=0) for g in groups]
        bkw = [jnp.concatenate([bw[:, gl[g]], kw[:, gl[g]]], axis=0) for g in groups]
        full = [lax.dot_general(uv[g], bkw[g], _TN, preferred_element_type=F32) for g in groups]
        for g in groups:
            upd = sum(jnp.where(lane_head == h, full[g][h * N:(h + 1) * N, :], 0.0) for h in range(GH))
            s_scr[g] = s_old[g] * w_end[:, gl[g]] + upd
        return carry

    lax.fori_loop(0, n_chunks, chunk, 0)

    @pl.when(j == pl.num_programs(1) - 1)
    def _():
        for g in groups:
            s = s_scr[g]
            for h in range(GH):
                sfin_ref[0, g * GH + h] = s[:, h * N:(h + 1) * N]


def _slab(t):
    return t if isinstance(t, tuple) else (t.reshape((1,) + t.shape), 0)


def _wkv_seq(r, k, v, lw, kkn, alr, k_a, s0, seq_len):
    srcs = [_slab(t) for t in (r, k, v, lw, kkn, alr)]
    _, M, DI = srcs[0][0].shape
    B = M // seq_len
    H = DI // A_HEAD
    tb = min(seq_len, ROW_TILE)
    nt = seq_len // tb
    blk = pl.BlockSpec((tb, DI), lambda bi, j: (bi * nt + j, 0))
    slab_blk = lambda p: pl.BlockSpec((1, tb, DI), lambda bi, j: (p, bi * nt + j, 0))
    sblk = pl.BlockSpec((1, H, A_HEAD, A_HEAD), lambda bi, j: (bi, 0, 0, 0))
    return pl.pallas_call(
        _wkv_body,
        grid=(B, nt),
        in_specs=[slab_blk(p) for _, p in srcs] + [pl.BlockSpec((1, DI), lambda bi, j: (0, 0)), sblk],
        out_specs=[blk, sblk],
        out_shape=[jax.ShapeDtypeStruct((M, DI), BF16), jax.ShapeDtypeStruct(s0.shape, F32)],
        scratch_shapes=[pltpu.VMEM((H // WKV_LANE_HEADS, A_HEAD, WKV_LANE_HEADS * A_HEAD), F32)],
        compiler_params=_params("arbitrary", "arbitrary"),
        name="rwkv_wkv_chunked",
    )(*(t for t, _ in srcs), k_a, s0)


WKV_TOK_HEADS = 2


def _wkv_tok_body(layer, s_ref, r_ref, lw_ref, k_ref, v_ref, kkn_ref, alr_ref, ka_ref, *rest):
    snew_ref, y_ref = rest[-2:]
    if snew_ref.shape[0] > 1:
        for other in range(snew_ref.shape[0]):
            if other != layer:
                snew_ref[other] = jnp.zeros(snew_ref.shape[1:], F32)
        snew_ref = snew_ref.at[layer:layer + 1]
    for i in range(s_ref.shape[1]):
        s = s_ref[0, i]
        kkn, alr = kkn_ref[i], alr_ref[i]
        k = k_ref[i] * (1.0 + (alr - 1.0) * ka_ref[i])
        sa = jnp.sum(s * (-kkn)[None], axis=1, keepdims=True)
        s_new = s * jnp.exp(lw_ref[i])[None] + sa * (kkn * alr)[None] + v_ref[i][:, None, :] * k[None]
        snew_ref[0, i] = s_new
        y_ref[i] = jnp.sum(s_new * r_ref[i][None], axis=1)


def _wkv_tok(layer, state_t, prev_out, r, lw, k, v, kkn, alr, k_a):
    L, H, N, _, B = state_t.shape
    hb = WKV_TOK_HEADS
    heads = lambda t: t.astype(F32).T.reshape(H, N, B)
    ka = jnp.broadcast_to(k_a.reshape(H, N, 1), (H, N, B))
    sspec = pl.BlockSpec((1, hb, N, N, B), lambda h: (layer, h, 0, 0, 0))
    vspec = pl.BlockSpec((hb, N, B), lambda h: (h, 0, 0))
    ins = [state_t] + [heads(t) for t in (r, lw, k, v, kkn, alr)] + [ka]
    specs = [sspec] + [vspec] * 7
    aliases = {}
    out_sspec = sspec
    if prev_out is None:
        out_sspec = pl.BlockSpec((L, hb, N, N, B), lambda h: (0, h, 0, 0, 0))
    else:
        aliases = {len(ins): 0}
        ins.append(prev_out)
        specs.append(pl.BlockSpec(memory_space=pl.ANY))
    s_out, y = pl.pallas_call(
        functools.partial(_wkv_tok_body, layer),
        grid=(H // hb,),
        in_specs=specs,
        out_specs=[out_sspec, vspec],
        out_shape=[jax.ShapeDtypeStruct(state_t.shape, F32), jax.ShapeDtypeStruct((H, N, B), F32)],
        input_output_aliases=aliases,
        compiler_params=_params("arbitrary"),
        name="rwkv_wkv_step",
    )(*ins)
    return y.reshape(H * N, B).T, s_out


def _rwkv_out_body(y_ref, r_ref, k_ref, v_ref, z_ref, alr_ref, x_ref, lw_ref, lb_ref, rk_ref, ka_ref,
                   ones_ref, wo_ref, o_ref):
    n = float(A_HEAD)
    ones_bd = ones_ref[...]
    y = y_ref[...].astype(F32)
    yc = y - _head_sums(y, ones_bd, True) / n
    var = _head_sums(yc * yc, ones_bd, False) / n
    yn = yc * lax.rsqrt(var + GN_EPS) * lw_ref[...] + lb_ref[...]
    r, k, v, z = (ref[0].astype(F32) for ref in (r_ref, k_ref, v_ref, z_ref))
    k = k * (1.0 + (alr_ref[...].astype(F32) - 1.0) * ka_ref[...])
    bonus = _head_sums(r * k * rk_ref[...], ones_bd, False) * v
    g = (yn + bonus) * (z * _sigmoid(z))
    o_ref[...] = x_ref[...] + jnp.dot(g.astype(BF16), wo_ref[...], preferred_element_type=F32)


def _rwkv_out(y, rkvz, v, alr, x, lnx_w, lnx_b, r_k, k_a, ones_bd, w_out):
    M, DI = y.shape
    D = x.shape[1]
    tm = min(M, ROW_TILE)
    row = lambda i: (i, 0)
    full = lambda i: (0, 0)
    slab_spec = lambda p: pl.BlockSpec((1, tm, DI), lambda i: (p, i, 0))
    v_in, v_slab = _slab(v)
    vec = pl.BlockSpec((1, DI), full)
    return pl.pallas_call(
        _rwkv_out_body,
        grid=(M // tm,),
        in_specs=[pl.BlockSpec((tm, DI), row), slab_spec(0), slab_spec(1), slab_spec(v_slab), slab_spec(3),
                  pl.BlockSpec((tm, DI), row), pl.BlockSpec((tm, D), row), vec, vec, vec, vec,
                  pl.BlockSpec(ones_bd.shape, full), pl.BlockSpec(w_out.shape, full)],
        out_specs=pl.BlockSpec((tm, D), row),
        out_shape=jax.ShapeDtypeStruct((M, D), F32),
        compiler_params=_params("arbitrary"),
        name="rwkv_out_proj",
    )(y, rkvz, rkvz, v_in, rkvz, alr, x, lnx_w, lnx_b, r_k, k_a, ones_bd, w_out)


def _mla_kv_body(c_lat, r_dim, x_ref, nw_ref, w_ref, kvn_ref, cos_ref, sin_ref, ckv_ref, kpe_ref):
    kv = _dot(_rms(x_ref[...], nw_ref[...]), w_ref[...])
    ckv_ref[...] = _rms(kv[:, :c_lat], kvn_ref[...])
    pe = kv[:, c_lat:c_lat + r_dim]
    pe_rot = kv[:, c_lat + LANES:c_lat + LANES + r_dim]
    kpe_ref[...] = pe * cos_ref[...] + pe_rot * sin_ref[...]


def _mla_kv(x, norm_w, w_ext, kv_norm_w, cos, sin, c_lat, r_dim, seq_len):
    M, D = x.shape
    tm = min(seq_len if seq_len > 1 else M, ROW_TILE)
    pos_tiles = max(seq_len // tm, 1)
    row = lambda i: (i, 0)
    full = lambda i: (0, 0)
    pos = lambda i: (i % pos_tiles, 0)
    return pl.pallas_call(
        functools.partial(_mla_kv_body, c_lat, r_dim),
        grid=(M // tm,),
        in_specs=[pl.BlockSpec((tm, D), row), pl.BlockSpec((1, D), full), pl.BlockSpec(w_ext.shape, full),
                  pl.BlockSpec((1, c_lat), full), pl.BlockSpec((tm, r_dim), pos), pl.BlockSpec((tm, r_dim), pos)],
        out_specs=[pl.BlockSpec((tm, c_lat), row), pl.BlockSpec((tm, r_dim), row)],
        out_shape=[jax.ShapeDtypeStruct((M, c_lat), F32), jax.ShapeDtypeStruct((M, r_dim), F32)],
        compiler_params=_params("arbitrary"),
        name="mla_shared_kv",
    )(x, norm_w, w_ext, kv_norm_w, cos, sin)


def _mla_expand_body(n_heads, d_nope, d_v, ckv_ref, kpe_ref, w_ref, k_ref, v_ref):
    kv = _dot(ckv_ref[...], w_ref[...]).astype(BF16)
    pe = kpe_ref[...].astype(BF16)
    v0 = n_heads * d_nope
    for h in range(n_heads):
        k_ref[h, :, :d_nope] = kv[:, h * d_nope:(h + 1) * d_nope]
        k_ref[h, :, d_nope:] = pe
        v_ref[h] = kv[:, v0 + h * d_v:v0 + (h + 1) * d_v]


def _mla_expand(ckv, kpe, w_ukv, n_heads, d_nope, d_v):
    M, c_lat = ckv.shape
    r_dim = kpe.shape[1]
    tm = min(M, ROW_TILE)
    row = lambda i: (i, 0)
    return pl.pallas_call(
        functools.partial(_mla_expand_body, n_heads, d_nope, d_v),
        grid=(M // tm,),
        in_specs=[pl.BlockSpec((tm, c_lat), row), pl.BlockSpec((tm, r_dim), row),
                  pl.BlockSpec(w_ukv.shape, lambda i: (0, 0))],
        out_specs=[pl.BlockSpec((n_heads, tm, d_nope + r_dim), lambda i: (0, i, 0)),
                   pl.BlockSpec((n_heads, tm, d_v), lambda i: (0, i, 0))],
        out_shape=[jax.ShapeDtypeStruct((n_heads, M, d_nope + r_dim), BF16),
                   jax.ShapeDtypeStruct((n_heads, M, d_v), BF16)],
        compiler_params=_params("arbitrary"),
        name="mla_expand_kv",
    )(ckv, kpe, w_ukv)


def _mla_q_body(n_heads, d_nope, r_dim, q_lat, scale,
                x_ref, nw_ref, win_ref, qn_ref, wuq_ref, cos_ref, sin_ref, q_ref, z_ref):
    proj = _dot(_rms(x_ref[...], nw_ref[...]), win_ref[...])
    z_ref[...] = proj[:, q_lat:].astype(z_ref.dtype)
    q = _dot(_rms(proj[:, :q_lat], qn_ref[...]), wuq_ref[...]) * scale
    n0 = n_heads * d_nope
    n1 = n0 + n_heads * r_dim
    pe = q[:, n0:n1] * cos_ref[...] + q[:, n1:] * sin_ref[...]
    for h in range(n_heads):
        q_ref[h, :, :d_nope] = q[:, h * d_nope:(h + 1) * d_nope].astype(BF16)
        q_ref[h, :, d_nope:] = pe[:, h * r_dim:(h + 1) * r_dim].astype(BF16)


def _mla_q(x, norm_w, w_in, q_norm_w, w_uq_ext, cos_q, sin_q, n_heads, d_nope, r_dim, scale, seq_len):
    M, D = x.shape
    q_lat = q_norm_w.shape[1]
    d_gate = w_in.shape[1] - q_lat
    tm = min(seq_len if seq_len > 1 else M, ROW_TILE)
    pos_tiles = max(seq_len // tm, 1)
    row = lambda i: (i, 0)
    full = lambda i: (0, 0)
    pos = lambda i: (i % pos_tiles, 0)
    return pl.pallas_call(
        functools.partial(_mla_q_body, n_heads, d_nope, r_dim, q_lat, scale),
        grid=(M // tm,),
        in_specs=[pl.BlockSpec((tm, D), row), pl.BlockSpec((1, D), full), pl.BlockSpec(w_in.shape, full),
                  pl.BlockSpec((1, q_lat), full), pl.BlockSpec(w_uq_ext.shape, full),
                  pl.BlockSpec((tm, n_heads * r_dim), pos), pl.BlockSpec((tm, n_heads * r_dim), pos)],
        out_specs=[pl.BlockSpec((n_heads, tm, d_nope + r_dim), lambda i: (0, i, 0)),
                   pl.BlockSpec((tm, d_gate), row)],
        out_shape=[jax.ShapeDtypeStruct((n_heads, M, d_nope + r_dim), BF16),
                   jax.ShapeDtypeStruct((M, d_gate), BF16)],
        compiler_params=_params("arbitrary"),
        name="mla_query",
    )(x, norm_w, w_in, q_norm_w, w_uq_ext, cos_q, sin_q)


def _flash_body(t, q_ref, k_ref, v_ref, o_ref):
    blocks = range(k_ref.shape[1] // t)
    causal = (lax.broadcasted_iota(jnp.int32, (t, t), 1) <= lax.broadcasted_iota(jnp.int32, (t, t), 0))
    blk = lambda ref, n: ref[0, n * t:(n + 1) * t, :]
    q = [blk(q_ref, n) for n in blocks]
    s_diag = [jnp.where(causal, lax.dot_general(q[n], blk(k_ref, n), _NT, preferred_element_type=F32), -jnp.inf)
              for n in blocks]
    s_past = [None] + [lax.dot_general(q[n], k_ref[0, :n * t, :], _NT, preferred_element_type=F32)
                       for n in blocks[1:]]
    m = [jnp.max(s_diag[n], axis=-1, keepdims=True) for n in blocks]
    m = [m[0]] + [jnp.maximum(m[n], jnp.max(s_past[n], axis=-1, keepdims=True)) for n in blocks[1:]]
    p_diag = [jnp.exp(s_diag[n] - m[n]) for n in blocks]
    p_past = [None] + [jnp.exp(s_past[n] - m[n]) for n in blocks[1:]]
    l = [jnp.sum(p_diag[n], axis=-1, keepdims=True) for n in blocks]
    l = [l[0]] + [l[n] + jnp.sum(p_past[n], axis=-1, keepdims=True) for n in blocks[1:]]
    o = [jnp.dot(p_diag[n].astype(BF16), blk(v_ref, n), preferred_element_type=F32) for n in blocks]
    o = [o[0]] + [o[n] + jnp.dot(p_past[n].astype(BF16), v_ref[0, :n * t, :], preferred_element_type=F32)
                  for n in blocks[1:]]
    for n in blocks:
        o_ref[n * t:(n + 1) * t, :] = (o[n] / l[n]).astype(o_ref.dtype)


def _flash(q, k, v, seq_len):
    H, M, dk = q.shape
    dv = v.shape[2]
    B = M // seq_len
    t = min(seq_len, FLASH_BLOCK)
    seq = lambda d: pl.BlockSpec((1, seq_len, d), lambda b, h: (h, b, 0))
    return pl.pallas_call(
        functools.partial(_flash_body, t),
        grid=(B, H),
        in_specs=[seq(dk), seq(dk), seq(dv)],
        out_specs=pl.BlockSpec((seq_len, dv), lambda b, h: (b, h)),
        out_shape=jax.ShapeDtypeStruct((M, H * dv), BF16),
        compiler_params=_params("arbitrary", "arbitrary"),
        name="mla_prompt_attention",
    )(q, k, v)


def _mla_out_body(with_norm, o_ref, z_ref, x_ref, wo_ref, fw_ref, xo_ref, *maybe_y):
    z = z_ref[...].astype(F32)
    g = o_ref[...].astype(F32) * (z * _sigmoid(z))
    xn = x_ref[...] + jnp.dot(g.astype(BF16), wo_ref[...], preferred_element_type=F32)
    xo_ref[...] = xn
    if with_norm:
        maybe_y[0][...] = _rms(xn, fw_ref[...])


def _mla_out(o, z, x, w_out, final_w, with_norm):
    M, DI = o.shape
    D = x.shape[1]
    tm = min(M, ROW_TILE)
    row = lambda i: (i, 0)
    full = lambda i: (0, 0)
    n_out = 2 if with_norm else 1
    outs = pl.pallas_call(
        functools.partial(_mla_out_body, with_norm),
        grid=(M // tm,),
        in_specs=[pl.BlockSpec((tm, DI), row), pl.BlockSpec((tm, DI), row), pl.BlockSpec((tm, D), row),
                  pl.BlockSpec(w_out.shape, full), pl.BlockSpec((1, D), full)],
        out_specs=[pl.BlockSpec((tm, D), row)] * n_out,
        out_shape=[jax.ShapeDtypeStruct((M, D), F32)] * n_out,
        compiler_params=_params("arbitrary"),
        name="mla_out_proj",
    )(o, z, x, w_out, final_w)
    return outs if with_norm else (outs[0], None)


def _headwise_mm_body(x_ref, w_ref, o_ref):
    o_ref[0] = jnp.dot(x_ref[0].astype(BF16), w_ref[0], preferred_element_type=F32).astype(o_ref.dtype)


def _headwise_mm(x, w, out_dtype, name):
    H, B, K = x.shape
    N = w.shape[2]
    return pl.pallas_call(
        _headwise_mm_body,
        grid=(H,),
        in_specs=[pl.BlockSpec((1, B, K), lambda h: (h, 0, 0)), pl.BlockSpec((1, K, N), lambda h: (h, 0, 0))],
        out_specs=pl.BlockSpec((1, B, N), lambda h: (h, 0, 0)),
        out_shape=jax.ShapeDtypeStruct((H, B, N), out_dtype),
        compiler_params=_params("arbitrary"),
        name=name,
    )(x, w)


DECODE_PAGES_PER_STEP = 64


def _decode_body(n_pg, pt_ref, ql_ref, qp_ref, cn_ref, pn_ref, *refs):
    ckv_refs = refs[:n_pg]
    kpe_refs = refs[n_pg:2 * n_pg]
    o_ref, m_scr, l_scr, acc_scr = refs[2 * n_pg:]
    j = pl.program_id(1)
    ql = ql_ref[0]
    qp = qp_ref[0]

    @pl.when(j == 0)
    def _():
        cn = cn_ref[0]
        s_new = (jnp.sum(ql.astype(F32) * cn, axis=-1, keepdims=True)
                 + jnp.sum(qp.astype(F32) * pn_ref[0], axis=-1, keepdims=True))
        m_scr[...] = jnp.broadcast_to(s_new, m_scr.shape)
        l_scr[...] = jnp.ones(l_scr.shape, F32)
        acc_scr[...] = jnp.broadcast_to(cn, acc_scr.shape)

    ckv = jnp.concatenate([r[0].astype(BF16) for r in ckv_refs], axis=0)
    kpe_t = jnp.concatenate([r[0].astype(BF16) for r in kpe_refs], axis=1)
    s = (lax.dot_general(ql, ckv, _NT, preferred_element_type=F32)
         + jnp.dot(qp, kpe_t, preferred_element_type=F32))
    m_old = m_scr[...]
    m_new = jnp.maximum(m_old, jnp.max(s, axis=-1, keepdims=True))
    alpha = jnp.exp(m_old - m_new)
    p = jnp.exp(s - m_new[:, :1])
    l_new = alpha * l_scr[...] + jnp.sum(p, axis=-1, keepdims=True)
    acc = alpha[:, :1] * acc_scr[...] + jnp.dot(p.astype(BF16), ckv, preferred_element_type=F32)
    m_scr[...] = m_new
    l_scr[...] = l_new
    acc_scr[...] = acc

    @pl.when(j == pl.num_programs(1) - 1)
    def _():
        o_ref[0] = acc / l_new[:, :1]


def _decode_attn(q_lat, q_pe, ckv_new, kpe_new, cache_ckv, cache_kpe, page_table):
    B, H, C = q_lat.shape
    R = q_pe.shape[2]
    page = cache_ckv.shape[1]
    n_pages = page_table.shape[1]
    n_pg = min(DECODE_PAGES_PER_STEP, n_pages)
    ckv_spec = lambda i: pl.BlockSpec((1, page, C), lambda b, j, pt: (pt[b, j * n_pg + i], 0, 0))
    kpe_spec = lambda i: pl.BlockSpec((1, R, page), lambda b, j, pt: (pt[b, j * n_pg + i], 0, 0))
    per_b = lambda b, j, pt: (b, 0, 0)
    grid_spec = pltpu.PrefetchScalarGridSpec(
        num_scalar_prefetch=1,
        grid=(B, n_pages // n_pg),
        in_specs=[pl.BlockSpec((1, H, C), per_b), pl.BlockSpec((1, H, R), per_b),
                  pl.BlockSpec((1, 1, C), per_b), pl.BlockSpec((1, 1, R), per_b)]
                 + [ckv_spec(i) for i in range(n_pg)] + [kpe_spec(i) for i in range(n_pg)],
        out_specs=pl.BlockSpec((1, H, C), per_b),
        scratch_shapes=[pltpu.VMEM((H, LANES), F32), pltpu.VMEM((H, LANES), F32), pltpu.VMEM((H, C), F32)],
    )
    return pl.pallas_call(
        functools.partial(_decode_body, n_pg),
        grid_spec=grid_spec,
        out_shape=jax.ShapeDtypeStruct((B, H, C), F32),
        compiler_params=_params("arbitrary", "arbitrary"),
        name="mla_sample_attention",
    )(page_table, q_lat, q_pe, ckv_new, kpe_new, *([cache_ckv] * n_pg), *([cache_kpe] * n_pg))


def _rot_half_cols(w, r_dim):
    lead = w.shape[:-1]
    wb = w.reshape(lead + (-1, 2, r_dim // 2))
    return jnp.stack([-wb[..., 1, :], wb[..., 0, :]], axis=-2).reshape(w.shape)


def _rope_tables(pos, r_dim):
    half = r_dim // 2
    inv_freq = ROPE_THETA ** (-jnp.arange(half, dtype=F32) / half)
    ang = pos.astype(F32)[:, None] * inv_freq[None, :]
    cos, sin = jnp.cos(ang), jnp.sin(ang)
    return jnp.concatenate([cos, cos], axis=1), jnp.concatenate([sin, sin], axis=1)


def kernel(x_prompt, x_sample, state_wkv, state_shift, cache_ckv, cache_kpe, page_table, a_norm_w, a_mix, a_w_in, a_w0, a_w1, a_w2, a_a0, a_a1, a_a2, a_v0, a_v1, a_v2, a_k_k, a_k_a, a_r_k, a_lnx_w, a_lnx_b, a_w_out, kv_in_norm_w, w_dkv, kv_norm_w, w_uk, w_uv, b_norm_w, b_w_in, b_q_norm_w, b_w_uq, b_w_out, final_norm_w):
    Bp, Sp, D = x_prompt.shape
    Bs, Ts, _ = x_sample.shape
    assert Ts == 1
    n_a = a_norm_w.shape[0]
    n_b = b_norm_w.shape[0]
    DI = a_w_in.shape[3]
    H = DI // A_HEAD
    c_lat, n_bh, d_nope = w_uk.shape
    d_v = w_uv.shape[2]
    r_dim = cache_kpe.shape[2]
    past = page_table.shape[1] * cache_ckv.shape[1]
    scale = float(d_nope + r_dim) ** -0.5
    bf = lambda t: t.astype(BF16)
    vec = lambda t: t.reshape(1, -1)

    head_of_lane = jnp.arange(DI, dtype=jnp.int32) // A_HEAD
    ones_bd = (head_of_lane[:HEAD_SUM_WIDTH, None] == head_of_lane[None, :HEAD_SUM_WIDTH]).astype(BF16)

    xp = x_prompt.reshape(Bp * Sp, D)
    xs = x_sample.reshape(Bs, D)
    shift0 = jnp.zeros((Bp, D), F32)
    wkv0 = jnp.zeros((Bp, H, A_HEAD, A_HEAD), F32)
    vf_p = vf_s = None
    wkv_p, sh_p, sh_s = [], [], []
    state_t = jnp.transpose(state_wkv, (0, 2, 3, 4, 1))
    wkv_s_t = None

    for l in range(n_a):
        w_in = bf(a_w_in[l])
        lw_w = (bf(a_w1[l]), bf(a_w2[l]), vec(a_w0[l]))
        lora_a = (bf(a_a1[l]), bf(a_a2[l]), vec(a_a0[l]))
        lora_v = None if l == 0 else (bf(a_v1[l - 1]), bf(a_v2[l - 1]), vec(a_v0[l - 1]))
        k_k, k_a = vec(a_k_k[l]), vec(a_k_a[l])
        lnx_w, lnx_b, r_k = vec(a_lnx_w[l]), vec(a_lnx_b[l]), vec(a_r_k[l])
        w_out = bf(a_w_out[l])

        def layer(x, prev, s0, v_first, seq_len):
            mixed, hlast = _premix(x, prev, a_norm_w[l], a_mix[l], seq_len)
            rkvz = _bmm(mixed, w_in, BF16, 4)
            outs = _lora(mixed, rkvz, v_first, lw_w, lora_a, lora_v, k_k, ones_bd)
            lw, kkn, alr = outs[:3]
            v = (rkvz, 2) if lora_v is None else outs[3]
            if seq_len == 1:
                v2d = rkvz[2] if lora_v is None else v
                y, s_fin = _wkv_tok(l, state_t, s0, rkvz[0], lw, rkvz[1], v2d, kkn, alr, k_a)
            else:
                y, s_fin = _wkv_seq((rkvz, 0), (rkvz, 1), v, lw, kkn, alr, k_a, s0, seq_len)
            x_new = _rwkv_out(y, rkvz, v, alr, x, lnx_w, lnx_b, r_k, k_a, ones_bd, w_out)
            return x_new, (rkvz, 2), s_fin, hlast

        xp, v, S, last = layer(xp, shift0, wkv0, vf_p, Sp)
        wkv_p.append(S)
        sh_p.append(last)
        if l == 0:
            vf_p = v
        xs, v, wkv_s_t, last = layer(xs, state_shift[l], wkv_s_t, vf_s, 1)
        sh_s.append(last)
        if l == 0:
            vf_s = v

    pad = jnp.zeros((D, LANES - r_dim), F32)
    w_pe = w_dkv[:, c_lat:]
    w_dkv_ext = bf(jnp.concatenate([w_dkv[:, :c_lat], w_pe, pad, _rot_half_cols(w_pe, r_dim), pad], axis=1))
    cos_p, sin_p = _rope_tables(jnp.arange(Sp, dtype=jnp.int32), r_dim)
    cos_s, sin_s = _rope_tables(jnp.full((Bs,), past, dtype=jnp.int32), r_dim)
    ckv_p, kpe_p = _mla_kv(xp, vec(kv_in_norm_w), w_dkv_ext, vec(kv_norm_w), cos_p, sin_p, c_lat, r_dim, Sp)
    ckv_s, kpe_s = _mla_kv(xs, vec(kv_in_norm_w), w_dkv_ext, vec(kv_norm_w), cos_s, sin_s, c_lat, r_dim, 1)
    w_ukv = bf(jnp.concatenate([w_uk.reshape(c_lat, n_bh * d_nope), w_uv.reshape(c_lat, n_bh * d_v)], axis=1))
    k_cat, v_heads = _mla_expand(ckv_p, kpe_p, w_ukv, n_bh, d_nope, d_v)
    w_uk_t = bf(jnp.transpose(w_uk, (1, 2, 0)))
    w_uv_h = bf(jnp.transpose(w_uv, (1, 0, 2)))
    cache_kpe_t = jnp.swapaxes(cache_kpe, 1, 2)
    tile_h = lambda t: jnp.tile(t, (1, n_bh))
    cos_pq, sin_pq, cos_sq, sin_sq = tile_h(cos_p), tile_h(sin_p), tile_h(cos_s), tile_h(sin_s)

    y_p = y_s = None
    for l in range(n_b):
        last_layer = l == n_b - 1
        w_in = bf(b_w_in[l])
        n0 = n_bh * d_nope
        uq = b_w_uq[l].reshape(-1, n_bh, d_nope + r_dim)
        uq_pe = uq[:, :, d_nope:].reshape(-1, n_bh * r_dim)
        w_uq_ext = bf(jnp.concatenate([uq[:, :, :d_nope].reshape(-1, n0), uq_pe, _rot_half_cols(uq_pe, r_dim)], axis=1))
        w_out = bf(b_w_out[l])
        nw, qnw, fw = vec(b_norm_w[l]), vec(b_q_norm_w[l]), vec(final_norm_w)

        q_cat, z = _mla_q(xp, nw, w_in, qnw, w_uq_ext, cos_pq, sin_pq, n_bh, d_nope, r_dim, scale, Sp)
        o = _flash(q_cat, k_cat, v_heads, Sp)
        xp, y_p = _mla_out(o, z, xp, w_out, fw, last_layer)

        q_cat, z = _mla_q(xs, nw, w_in, qnw, w_uq_ext, cos_sq, sin_sq, n_bh, d_nope, r_dim, scale, 1)
        q_lat = _headwise_mm(q_cat[:, :, :d_nope], w_uk_t, BF16, "mla_absorb_q")
        o_lat = _decode_attn(jnp.transpose(q_lat, (1, 0, 2)), jnp.transpose(q_cat[:, :, d_nope:], (1, 0, 2)),
                             ckv_s.reshape(Bs, 1, c_lat), kpe_s.reshape(Bs, 1, r_dim),
                             cache_ckv, cache_kpe_t, page_table)
        o_h = _headwise_mm(jnp.transpose(o_lat, (1, 0, 2)), w_uv_h, F32, "mla_value_up")
        o = jnp.transpose(o_h, (1, 0, 2)).reshape(Bs, n_bh * d_v)
        xs, y_s = _mla_out(o, z, xs, w_out, fw, last_layer)

    return (y_p.reshape(Bp, Sp, D), y_s.reshape(Bs, Ts, D),
            jnp.stack(wkv_p), jnp.stack(sh_p), ckv_p.reshape(Bp, Sp, c_lat), kpe_p.reshape(Bp, Sp, r_dim),
            jnp.transpose(wkv_s_t, (0, 4, 1, 2, 3)), jnp.stack(sh_s),
            ckv_s.reshape(Bs, Ts, c_lat), kpe_s.reshape(Bs, Ts, r_dim))
```

```python
import functools
import math

import jax
import jax.numpy as jnp
from jax import lax
from jax.experimental import pallas as pl
from jax.experimental.pallas import tpu as pltpu

F32 = jnp.float32
BF16 = jnp.bfloat16

NORM_EPS = 1e-6
ROPE_THETA = 10000.0
A_HEAD = 64
GN_EPS = A_HEAD * 1e-5
WKV_CHUNK = 64
WKV_LANE_HEADS = 4
HEAD_SUM_WIDTH = 256
FLASH_BLOCK = 512
LANES = 128
ROW_TILE = 256
V7X_VMEM_LIMIT = 48 * 1024 * 1024

_NT = (((1,), (1,)), ((), ()))
_TN = (((0,), (0,)), ((), ()))


def _params(*sem):
    return pltpu.CompilerParams(dimension_semantics=sem, vmem_limit_bytes=V7X_VMEM_LIMIT)


def _dot(a, b):
    return jnp.dot(a.astype(BF16), b.astype(BF16), preferred_element_type=F32)


def _split2(x):
    hi = x.astype(BF16)
    lo = (x - hi.astype(F32)).astype(BF16)
    return hi, lo


def _head_sums(t, ones_bd, two_pass):
    w = ones_bd.shape[0]
    cols = []
    for i in range(t.shape[1] // w):
        blk = t[:, i * w:(i + 1) * w]
        if two_pass:
            hi, lo = _split2(blk)
            cols.append(jnp.dot(hi, ones_bd, preferred_element_type=F32)
                        + jnp.dot(lo, ones_bd, preferred_element_type=F32))
        else:
            cols.append(jnp.dot(blk.astype(BF16), ones_bd, preferred_element_type=F32))
    return jnp.concatenate(cols, axis=1)


def _sigmoid(x):
    return 1.0 / (1.0 + jnp.exp(-x))


def _rms(x, w):
    return x * lax.rsqrt(jnp.mean(x * x, axis=-1, keepdims=True) + NORM_EPS) * w


def _premix_seq_body(seq_tiles, x_ref, prev_ref, nw_ref, mix_ref, mixed_ref, hlast_ref, carry_ref):
    i = pl.program_id(0)
    h = _rms(x_ref[...], nw_ref[...])
    tm = h.shape[0]
    prev_row = jnp.where(i % seq_tiles == 0, prev_ref[0], carry_ref[...])
    row = lax.broadcasted_iota(jnp.int32, h.shape, 0)
    shifted = jnp.where(row == 0, prev_row, pltpu.roll(h, 1, axis=0))
    last = h[tm - 1:tm, :]
    carry_ref[...] = last
    hlast_ref[0] = last
    dx = shifted - h
    for p in range(6):
        mixed_ref[p] = (h + dx * mix_ref[p:p + 1, :]).astype(BF16)


def _premix_tok_body(x_ref, prev_ref, nw_ref, mix_ref, mixed_ref, hlast_ref):
    h = _rms(x_ref[...], nw_ref[...])
    hlast_ref[...] = h
    dx = prev_ref[...] - h
    for p in range(6):
        mixed_ref[p] = (h + dx * mix_ref[p:p + 1, :]).astype(BF16)


def _premix(x, prev, norm_w, mix, seq_len):
    M, D = x.shape
    B = M // seq_len
    nw = norm_w.reshape(1, D)
    out_mixed = jax.ShapeDtypeStruct((6, M, D), BF16)
    if seq_len == 1:
        tm = min(M, ROW_TILE)
        return pl.pallas_call(
            _premix_tok_body,
            grid=(M // tm,),
            in_specs=[pl.BlockSpec((tm, D), lambda i: (i, 0)),
                      pl.BlockSpec((tm, D), lambda i: (i, 0)),
                      pl.BlockSpec((1, D), lambda i: (0, 0)),
                      pl.BlockSpec((6, D), lambda i: (0, 0))],
            out_specs=[pl.BlockSpec((6, tm, D), lambda i: (0, i, 0)),
                       pl.BlockSpec((tm, D), lambda i: (i, 0))],
            out_shape=[out_mixed, jax.ShapeDtypeStruct((B, D), F32)],
            compiler_params=_params("arbitrary"),
            name="rwkv_premix_tok",
        )(x, prev, nw, mix)
    tm = min(seq_len, ROW_TILE)
    seq_tiles = seq_len // tm
    mixed, hlast = pl.pallas_call(
        functools.partial(_premix_seq_body, seq_tiles),
        grid=(M // tm,),
        in_specs=[pl.BlockSpec((tm, D), lambda i: (i, 0)),
                  pl.BlockSpec((1, 1, D), lambda i: (i // seq_tiles, 0, 0)),
                  pl.BlockSpec((1, D), lambda i: (0, 0)),
                  pl.BlockSpec((6, D), lambda i: (0, 0))],
        out_specs=[pl.BlockSpec((6, tm, D), lambda i: (0, i, 0)),
                   pl.BlockSpec((1, 1, D), lambda i: (i // seq_tiles, 0, 0))],
        out_shape=[out_mixed, jax.ShapeDtypeStruct((B, 1, D), F32)],
        scratch_shapes=[pltpu.VMEM((1, D), F32)],
        compiler_params=_params("arbitrary"),
        name="rwkv_premix_seq",
    )(x, prev.reshape(B, 1, D), nw, mix)
    return mixed, hlast.reshape(B, D)


def _bmm_body(x_ref, w_ref, o_ref):
    o_ref[0] = jnp.dot(x_ref[0], w_ref[0], preferred_element_type=F32).astype(o_ref.dtype)


def _bmm(x, w, out_dtype, n_batch, tm=512, tn=2048):
    _, M, K = x.shape
    P, _, N = w.shape
    assert P == n_batch
    tm, tn = min(tm, M), min(tn, N)
    return pl.pallas_call(
        _bmm_body,
        grid=(P, N // tn, M // tm),
        in_specs=[pl.BlockSpec((1, tm, K), lambda p, n, m: (p, m, 0)),
                  pl.BlockSpec((1, K, tn), lambda p, n, m: (p, 0, n))],
        out_specs=pl.BlockSpec((1, tm, tn), lambda p, n, m: (p, m, n)),
        out_shape=jax.ShapeDtypeStruct((P, M, N), out_dtype),
        compiler_params=_params("arbitrary", "arbitrary", "arbitrary"),
        name="rwkv_in_proj",
    )(x, w)


def _lora_body(has_vres, *refs):
    if has_vres:
        (xw_ref, xa_ref, xv_ref, k_ref, v_ref, vf_ref,
         w1_ref, w2_ref, w0_ref, a1_ref, a2_ref, a0_ref, v1_ref, v2_ref, v0_ref,
         kk_ref, ones_ref,
         lw_ref, kkn_ref, alr_ref, vp_ref) = refs
    else:
        (xw_ref, xa_ref, k_ref,
         w1_ref, w2_ref, w0_ref, a1_ref, a2_ref, a0_ref,
         kk_ref, ones_ref,
         lw_ref, kkn_ref, alr_ref) = refs
    xd = w0_ref[...] + _dot(jnp.tanh(_dot(xw_ref[0], w1_ref[...])), w2_ref[...])
    lw_ref[...] = -math.exp(-0.5) * _sigmoid(xd)
    alr_ref[...] = _sigmoid(a0_ref[...] + _dot(_dot(xa_ref[0], a1_ref[...]), a2_ref[...])).astype(BF16)
    kk = k_ref[0].astype(F32) * kk_ref[...]
    kkn_ref[...] = (kk / jnp.maximum(jnp.sqrt(_head_sums(kk * kk, ones_ref[...], True)), 1e-12)).astype(BF16)
    if has_vres:
        v = v_ref[0].astype(F32)
        vg = _sigmoid(v0_ref[...] + _dot(_dot(xv_ref[0], v1_ref[...]), v2_ref[...]))
        vp_ref[...] = (v + (vf_ref[0].astype(F32) - v) * vg).astype(BF16)


def _lora(mixed, rkvz, v_first, lw_w, lora_a, lora_v, k_k, ones_bd):
    _, M, D = mixed.shape
    DI = rkvz.shape[2]
    tm = min(M, ROW_TILE)
    has_vres = lora_v is not None
    row = lambda i: (i, 0)
    full = lambda i: (0, 0)
    mixed_spec = lambda p: pl.BlockSpec((1, tm, D), lambda i: (p, i, 0))
    rkvz_spec = lambda p: pl.BlockSpec((1, tm, DI), lambda i: (p, i, 0))
    wspec = lambda a: pl.BlockSpec(a.shape, full)
    ins = [mixed, mixed]
    specs = [mixed_spec(4), mixed_spec(5)]
    if has_vres:
        ins += [mixed]
        specs += [mixed_spec(2)]
    ins += [rkvz]
    specs += [rkvz_spec(1)]
    if has_vres:
        vf, vf_slab = _slab(v_first)
        ins += [rkvz, vf]
        specs += [rkvz_spec(2), pl.BlockSpec((1, tm, DI), lambda i: (vf_slab, i, 0))]
    weights = list(lw_w) + list(lora_a) + (list(lora_v) if has_vres else []) + [k_k, ones_bd]
    ins += weights
    specs += [wspec(a) for a in weights]
    out_dtypes = [F32, BF16, BF16] + ([BF16] if has_vres else [])
    return pl.pallas_call(
        functools.partial(_lora_body, has_vres),
        grid=(M // tm,),
        in_specs=specs,
        out_specs=[pl.BlockSpec((tm, DI), row)] * len(out_dtypes),
        out_shape=[jax.ShapeDtypeStruct((M, DI), dt) for dt in out_dtypes],
        compiler_params=_params("arbitrary"),
        name="rwkv_lora",
    )(*ins)


def _wkv_body(r_ref, k_ref, v_ref, lw_ref, kkn_ref, alr_ref, ka_ref, s0_ref, y_ref, sfin_ref, s_scr):
    j = pl.program_id(1)
    C = WKV_CHUNK
    N = A_HEAD
    GH = WKV_LANE_HEADS
    GL = GH * N
    assert C == N
    shift = N.bit_length() - 1
    groups = range(s_scr.shape[0])
    n_chunks = r_ref.shape[1] // C

    @pl.when(j == 0)
    def _():
        for g in groups:
            s_scr[g] = jnp.concatenate([s0_ref[0, g * GH + h] for h in range(GH)], axis=1)

    row = lax.broadcasted_iota(jnp.int32, (C, GL), 0)
    lane = lax.broadcasted_iota(jnp.int32, (C, GL), 1)
    lane_in_head = jnp.bitwise_and(lane, N - 1)
    lane_head = jnp.right_shift(lane, shift)
    strict = row > lane_in_head
    incl = row >= lane_in_head
    row2 = lax.broadcasted_iota(jnp.int32, (2 * C, GL), 0)
    lane2 = jnp.bitwise_and(lax.broadcasted_iota(jnp.int32, (2 * C, GL), 1), N - 1)
    strict_incl = row2 >= jnp.where(row2 < C, lane2 + 1, lane2 + C)
    eye = (row == lane_in_head).astype(F32)
    tri = (lax.broadcasted_iota(jnp.int32, (C, C), 0) >= lax.broadcasted_iota(jnp.int32, (C, C), 1)).astype(BF16)
    blockdiag = (jnp.right_shift(lax.broadcasted_iota(jnp.int32, (GL, GL), 0), shift)
                 == jnp.right_shift(lax.broadcasted_iota(jnp.int32, (GL, GL), 1), shift)).astype(BF16)

    def bd(x):
        return jnp.concatenate([x.astype(BF16)] * GH, axis=0) * blockdiag

    def dotf(a, b):
        return jnp.dot(a, b, preferred_element_type=F32)

    def chunk(c, carry):
        rows = pl.ds(pl.multiple_of(c * C, C), C)
        lw = lw_ref[0, rows, :]
        l_hi, l_lo = _split2(lw)
        cs = dotf(tri, l_hi) + dotf(tri, l_lo)
        cs_last = cs[C - 1:C, :]
        kkn, alr, k, r, v = (ref[0, rows, :].astype(F32) for ref in (kkn_ref, alr_ref, k_ref, r_ref, v_ref))
        a = -kkn
        b = kkn * alr
        k = k * (1.0 + (alr - 1.0) * ka_ref[...])
        e_neg = jnp.exp(-cs)
        e_rem = jnp.exp(cs_last - cs)
        at = (a * jnp.exp(cs - lw)).astype(BF16)
        rt = (r * jnp.exp(cs)).astype(BF16)
        bt = (b * e_neg).astype(BF16)
        kt = (k * e_neg).astype(BF16)
        bw = (b * e_rem).astype(BF16)
        kw = (k * e_rem).astype(BF16)
        vb = v.astype(BF16)
        w_end = jnp.exp(cs_last)
        gl = [slice(g * GL, (g + 1) * GL) for g in groups]
        ar = [jnp.concatenate([at[:, gl[g]], rt[:, gl[g]]], axis=0) for g in groups]
        bk = [jnp.concatenate([bd(bt[:, gl[g]]), bd(kt[:, gl[g]])], axis=0) for g in groups]
        p = [lax.dot_general(ar[g], bk[g], _NT, preferred_element_type=F32) for g in groups]
        low = [jnp.where(strict, p[g][:C, :GL], 0.0) for g in groups]
        m_k = [jnp.where(strict_incl, p[g][:, GL:], 0.0).astype(BF16) for g in groups]
        m_rb = [jnp.where(incl, p[g][C:, :GL], 0.0).astype(BF16) for g in groups]
        s_old = [s_scr[g] for g in groups]
        ay0 = [lax.dot_general(ar[g], bd(s_old[g]), _NT, preferred_element_type=F32) for g in groups]
        kv = [dotf(m_k[g], bd(vb[:, gl[g]])) for g in groups]
        r0 = [ay0[g][:C] + kv[g][:C] for g in groups]
        x = [dotf(low[g].astype(BF16), bd(low[g])) for g in groups]
        acc = [eye + low[g] for g in groups]
        span = 2
        while 2 * span < C:
            xa = [dotf(jnp.concatenate([x[g], acc[g]], axis=0).astype(BF16), bd(x[g])) for g in groups]
            x = [xa[g][:C] for g in groups]
            acc = [acc[g] + xa[g][C:] for g in groups]
            span *= 2
        inv = [acc[g] + dotf(acc[g].astype(BF16), bd(x[g])) for g in groups]
        ub = [dotf(inv[g].astype(BF16), bd(r0[g])).astype(BF16) for g in groups]
        for g in groups:
            y_ref[rows, gl[g]] = (ay0[g][C:] + kv[g][C:] + dotf(m_rb[g], bd(ub[g]))).astype(y_ref.dtype)
        uv = [jnp.concatenate([ub[g], vb[:, gl[g]]], axis=0) for g in groups]
        bkw = [jnp.concatenate([bw[:, gl[g]], kw[:, gl[g]]], axis=0) for g in groups]
        full = [lax.dot_general(uv[g], bkw[g], _TN, preferred_element_type=F32) for g in groups]
        for g in groups:
            upd = sum(jnp.where(lane_head == h, full[g][h * N:(h + 1) * N, :], 0.0) for h in range(GH))
            s_scr[g] = s_old[g] * w_end[:, gl[g]] + upd
        return carry

    lax.fori_loop(0, n_chunks, chunk, 0)

    @pl.when(j == pl.num_programs(1) - 1)
    def _():
        for g in groups:
            s = s_scr[g]
            for h in range(GH):
                sfin_ref[0, g * GH + h] = s[:, h * N:(h + 1) * N]


def _slab(t):
    return t if isinstance(t, tuple) else (t.reshape((1,) + t.shape), 0)


def _wkv_seq(r, k, v, lw, kkn, alr, k_a, s0, seq_len):
    srcs = [_slab(t) for t in (r, k, v, lw, kkn, alr)]
    _, M, DI = srcs[0][0].shape
    B = M // seq_len
    H = DI // A_HEAD
    tb = min(seq_len, ROW_TILE)
    nt = seq_len // tb
    blk = pl.BlockSpec((tb, DI), lambda bi, j: (bi * nt + j, 0))
    slab_blk = lambda p: pl.BlockSpec((1, tb, DI), lambda bi, j: (p, bi * nt + j, 0))
    sblk = pl.BlockSpec((1, H, A_HEAD, A_HEAD), lambda bi, j: (bi, 0, 0, 0))
    return pl.pallas_call(
        _wkv_body,
        grid=(B, nt),
        in_specs=[slab_blk(p) for _, p in srcs] + [pl.BlockSpec((1, DI), lambda bi, j: (0, 0)), sblk],
        out_specs=[blk, sblk],
        out_shape=[jax.ShapeDtypeStruct((M, DI), BF16), jax.ShapeDtypeStruct(s0.shape, F32)],
        scratch_shapes=[pltpu.VMEM((H // WKV_LANE_HEADS, A_HEAD, WKV_LANE_HEADS * A_HEAD), F32)],
        compiler_params=_params("arbitrary", "arbitrary"),
        name="rwkv_wkv_chunked",
    )(*(t for t, _ in srcs), k_a, s0)


WKV_TOK_HEADS = 2


def _wkv_tok_body(layer, s_ref, r_ref, lw_ref, k_ref, v_ref, kkn_ref, alr_ref, ka_ref, *rest):
    snew_ref, y_ref = rest[-2:]
    if snew_ref.shape[0] > 1:
        for other in range(snew_ref.shape[0]):
            if other != layer:
                snew_ref[other] = jnp.zeros(snew_ref.shape[1:], F32)
        snew_ref = snew_ref.at[layer:layer + 1]
    for i in range(s_ref.shape[1]):
        s = s_ref[0, i]
        kkn, alr = kkn_ref[i], alr_ref[i]
        k = k_ref[i] * (1.0 + (alr - 1.0) * ka_ref[i])
        sa = jnp.sum(s * (-kkn)[None], axis=1, keepdims=True)
        s_new = s * jnp.exp(lw_ref[i])[None] + sa * (kkn * alr)[None] + v_ref[i][:, None, :] * k[None]
        snew_ref[0, i] = s_new
        y_ref[i] = jnp.sum(s_new * r_ref[i][None], axis=1)


def _wkv_tok(layer, state_t, prev_out, r, lw, k, v, kkn, alr, k_a):
    L, H, N, _, B = state_t.shape
    hb = WKV_TOK_HEADS
    heads = lambda t: t.astype(F32).T.reshape(H, N, B)
    ka = jnp.broadcast_to(k_a.reshape(H, N, 1), (H, N, B))
    sspec = pl.BlockSpec((1, hb, N, N, B), lambda h: (layer, h, 0, 0, 0))
    vspec = pl.BlockSpec((hb, N, B), lambda h: (h, 0, 0))
    ins = [state_t] + [heads(t) for t in (r, lw, k, v, kkn, alr)] + [ka]
    specs = [sspec] + [vspec] * 7
    aliases = {}
    out_sspec = sspec
    if prev_out is None:
        out_sspec = pl.BlockSpec((L, hb, N, N, B), lambda h: (0, h, 0, 0, 0))
    else:
        aliases = {len(ins): 0}
        ins.append(prev_out)
        specs.append(pl.BlockSpec(memory_space=pl.ANY))
    s_out, y = pl.pallas_call(
        functools.partial(_wkv_tok_body, layer),
        grid=(H // hb,),
        in_specs=specs,
        out_specs=[out_sspec, vspec],
        out_shape=[jax.ShapeDtypeStruct(state_t.shape, F32), jax.ShapeDtypeStruct((H, N, B), F32)],
        input_output_aliases=aliases,
        compiler_params=_params("arbitrary"),
        name="rwkv_wkv_step",
    )(*ins)
    return y.reshape(H * N, B).T, s_out


def _rwkv_out_body(y_ref, r_ref, k_ref, v_ref, z_ref, alr_ref, x_ref, lw_ref, lb_ref, rk_ref, ka_ref,
                   ones_ref, wo_ref, o_ref):
    n = float(A_HEAD)
    ones_bd = ones_ref[...]
    y = y_ref[...].astype(F32)
    yc = y - _head_sums(y, ones_bd, True) / n
    var = _head_sums(yc * yc, ones_bd, False) / n
    yn = yc * lax.rsqrt(var + GN_EPS) * lw_ref[...] + lb_ref[...]
    r, k, v, z = (ref[0].astype(F32) for ref in (r_ref, k_ref, v_ref, z_ref))
    k = k * (1.0 + (alr_ref[...].astype(F32) - 1.0) * ka_ref[...])
    bonus = _head_sums(r * k * rk_ref[...], ones_bd, False) * v
    g = (yn + bonus) * (z * _sigmoid(z))
    o_ref[...] = x_ref[...] + jnp.dot(g.astype(BF16), wo_ref[...], preferred_element_type=F32)


def _rwkv_out(y, rkvz, v, alr, x, lnx_w, lnx_b, r_k, k_a, ones_bd, w_out):
    M, DI = y.shape
    D = x.shape[1]
    tm = min(M, ROW_TILE)
    row = lambda i: (i, 0)
    full = lambda i: (0, 0)
    slab_spec = lambda p: pl.BlockSpec((1, tm, DI), lambda i: (p, i, 0))
    v_in, v_slab = _slab(v)
    vec = pl.BlockSpec((1, DI), full)
    return pl.pallas_call(
        _rwkv_out_body,
        grid=(M // tm,),
        in_specs=[pl.BlockSpec((tm, DI), row), slab_spec(0), slab_spec(1), slab_spec(v_slab), slab_spec(3),
                  pl.BlockSpec((tm, DI), row), pl.BlockSpec((tm, D), row), vec, vec, vec, vec,
                  pl.BlockSpec(ones_bd.shape, full), pl.BlockSpec(w_out.shape, full)],
        out_specs=pl.BlockSpec((tm, D), row),
        out_shape=jax.ShapeDtypeStruct((M, D), F32),
        compiler_params=_params("arbitrary"),
        name="rwkv_out_proj",
    )(y, rkvz, rkvz, v_in, rkvz, alr, x, lnx_w, lnx_b, r_k, k_a, ones_bd, w_out)


def _mla_kv_body(c_lat, r_dim, x_ref, nw_ref, w_ref, kvn_ref, cos_ref, sin_ref, ckv_ref, kpe_ref):
    kv = _dot(_rms(x_ref[...], nw_ref[...]), w_ref[...])
    ckv_ref[...] = _rms(kv[:, :c_lat], kvn_ref[...])
    pe = kv[:, c_lat:c_lat + r_dim]
    pe_rot = kv[:, c_lat + LANES:c_lat + LANES + r_dim]
    kpe_ref[...] = pe * cos_ref[...] + pe_rot * sin_ref[...]


def _mla_kv(x, norm_w, w_ext, kv_norm_w, cos, sin, c_lat, r_dim, seq_len):
    M, D = x.shape
    tm = min(seq_len if seq_len > 1 else M, ROW_TILE)
    pos_tiles = max(seq_len // tm, 1)
    row = lambda i: (i, 0)
    full = lambda i: (0, 0)
    pos = lambda i: (i % pos_tiles, 0)
    return pl.pallas_call(
        functools.partial(_mla_kv_body, c_lat, r_dim),
        grid=(M // tm,),
        in_specs=[pl.BlockSpec((tm, D), row), pl.BlockSpec((1, D), full), pl.BlockSpec(w_ext.shape, full),
                  pl.BlockSpec((1, c_lat), full), pl.BlockSpec((tm, r_dim), pos), pl.BlockSpec((tm, r_dim), pos)],
        out_specs=[pl.BlockSpec((tm, c_lat), row), pl.BlockSpec((tm, r_dim), row)],
        out_shape=[jax.ShapeDtypeStruct((M, c_lat), F32), jax.ShapeDtypeStruct((M, r_dim), F32)],
        compiler_params=_params("arbitrary"),
        name="mla_shared_kv",
    )(x, norm_w, w_ext, kv_norm_w, cos, sin)


def _mla_expand_body(n_heads, d_nope, d_v, ckv_ref, kpe_ref, wk_ref, wvt_ref, k_ref, vt_ref):
    ckv = ckv_ref[...].astype(BF16)
    k_nope = jnp.dot(ckv, wk_ref[...], preferred_element_type=F32).astype(BF16)
    v_t = lax.dot_general(wvt_ref[...], ckv, _NT, preferred_element_type=F32).astype(BF16)
    pe = kpe_ref[...].astype(BF16)
    for h in range(n_heads):
        k_ref[h, :, :d_nope] = k_nope[:, h * d_nope:(h + 1) * d_nope]
        k_ref[h, :, d_nope:] = pe
        vt_ref[h] = v_t[h * d_v:(h + 1) * d_v, :]


def _mla_expand(ckv, kpe, w_uk, w_uv_t, n_heads, d_nope, d_v):
    M, c_lat = ckv.shape
    r_dim = kpe.shape[1]
    tm = min(M, ROW_TILE)
    row = lambda i: (i, 0)
    full = lambda i: (0, 0)
    return pl.pallas_call(
        functools.partial(_mla_expand_body, n_heads, d_nope, d_v),
        grid=(M // tm,),
        in_specs=[pl.BlockSpec((tm, c_lat), row), pl.BlockSpec((tm, r_dim), row),
                  pl.BlockSpec(w_uk.shape, full), pl.BlockSpec(w_uv_t.shape, full)],
        out_specs=[pl.BlockSpec((n_heads, tm, d_nope + r_dim), lambda i: (0, i, 0)),
                   pl.BlockSpec((n_heads, d_v, tm), lambda i: (0, 0, i))],
        out_shape=[jax.ShapeDtypeStruct((n_heads, M, d_nope + r_dim), BF16),
                   jax.ShapeDtypeStruct((n_heads, d_v, M), BF16)],
        compiler_params=_params("arbitrary"),
        name="mla_expand_kv",
    )(ckv, kpe, w_uk, w_uv_t)


def _mla_q_body(n_heads, d_nope, r_dim, q_lat, scale,
                x_ref, nw_ref, win_ref, qn_ref, wuq_ref, cos_ref, sin_ref, q_ref, z_ref):
    proj = _dot(_rms(x_ref[...], nw_ref[...]), win_ref[...])
    z_ref[...] = proj[:, q_lat:].astype(z_ref.dtype)
    q = _dot(_rms(proj[:, :q_lat], qn_ref[...]), wuq_ref[...]) * scale
    n0 = n_heads * d_nope
    n1 = n0 + n_heads * r_dim
    pe = q[:, n0:n1] * cos_ref[...] + q[:, n1:] * sin_ref[...]
    for h in range(n_heads):
        q_ref[h, :, :d_nope] = q[:, h * d_nope:(h + 1) * d_nope].astype(BF16)
        q_ref[h, :, d_nope:] = pe[:, h * r_dim:(h + 1) * r_dim].astype(BF16)


def _mla_q(x, norm_w, w_in, q_norm_w, w_uq_ext, cos_q, sin_q, n_heads, d_nope, r_dim, scale, seq_len):
    M, D = x.shape
    q_lat = q_norm_w.shape[1]
    d_gate = w_in.shape[1] - q_lat
    tm = min(seq_len if seq_len > 1 else M, ROW_TILE)
    pos_tiles = max(seq_len // tm, 1)
    row = lambda i: (i, 0)
    full = lambda i: (0, 0)
    pos = lambda i: (i % pos_tiles, 0)
    return pl.pallas_call(
        functools.partial(_mla_q_body, n_heads, d_nope, r_dim, q_lat, scale),
        grid=(M // tm,),
        in_specs=[pl.BlockSpec((tm, D), row), pl.BlockSpec((1, D), full), pl.BlockSpec(w_in.shape, full),
                  pl.BlockSpec((1, q_lat), full), pl.BlockSpec(w_uq_ext.shape, full),
                  pl.BlockSpec((tm, n_heads * r_dim), pos), pl.BlockSpec((tm, n_heads * r_dim), pos)],
        out_specs=[pl.BlockSpec((n_heads, tm, d_nope + r_dim), lambda i: (0, i, 0)),
                   pl.BlockSpec((tm, d_gate), row)],
        out_shape=[jax.ShapeDtypeStruct((n_heads, M, d_nope + r_dim), BF16),
                   jax.ShapeDtypeStruct((M, d_gate), BF16)],
        compiler_params=_params("arbitrary"),
        name="mla_query",
    )(x, norm_w, w_in, q_norm_w, w_uq_ext, cos_q, sin_q)


def _flash_body(t, q_ref, k_ref, vt_ref, o_ref):
    blocks = range(k_ref.shape[1] // t)
    causal = (lax.broadcasted_iota(jnp.int32, (t, t), 0) <= lax.broadcasted_iota(jnp.int32, (t, t), 1))
    blk = lambda ref, n: ref[0, n * t:(n + 1) * t, :]
    q = [blk(q_ref, n) for n in blocks]
    s_diag = [jnp.where(causal, lax.dot_general(blk(k_ref, n), q[n], _NT, preferred_element_type=F32), -jnp.inf)
              for n in blocks]
    s_past = [None] + [lax.dot_general(k_ref[0, :n * t, :], q[n], _NT, preferred_element_type=F32)
                       for n in blocks[1:]]
    m = [jnp.max(s_diag[n], axis=0, keepdims=True) for n in blocks]
    m = [m[0]] + [jnp.maximum(m[n], jnp.max(s_past[n], axis=0, keepdims=True)) for n in blocks[1:]]
    p_diag = [jnp.exp(s_diag[n] - m[n]) for n in blocks]
    p_past = [None] + [jnp.exp(s_past[n] - m[n]) for n in blocks[1:]]
    l = [jnp.sum(p_diag[n], axis=0, keepdims=True) for n in blocks]
    l = [l[0]] + [l[n] + jnp.sum(p_past[n], axis=0, keepdims=True) for n in blocks[1:]]
    o = [jnp.dot(vt_ref[0, :, n * t:(n + 1) * t], p_diag[n].astype(BF16), preferred_element_type=F32)
         for n in blocks]
    o = [o[0]] + [o[n] + jnp.dot(vt_ref[0, :, :n * t], p_past[n].astype(BF16), preferred_element_type=F32)
                  for n in blocks[1:]]
    for n in blocks:
        o_ref[n * t:(n + 1) * t, :] = (o[n] / l[n]).T.astype(o_ref.dtype)


def _flash(q, k, vt, seq_len):
    H, M, dk = q.shape
    dv = vt.shape[1]
    B = M // seq_len
    t = min(seq_len, FLASH_BLOCK)
    seq = lambda d: pl.BlockSpec((1, seq_len, d), lambda b, h: (h, b, 0))
    return pl.pallas_call(
        functools.partial(_flash_body, t),
        grid=(B, H),
        in_specs=[seq(dk), seq(dk), pl.BlockSpec((1, dv, seq_len), lambda b, h: (h, 0, b))],
        out_specs=pl.BlockSpec((seq_len, dv), lambda b, h: (b, h)),
        out_shape=jax.ShapeDtypeStruct((M, H * dv), BF16),
        compiler_params=_params("arbitrary", "arbitrary"),
        name="mla_prompt_attention",
    )(q, k, vt)


def _mla_out_body(with_norm, o_ref, z_ref, x_ref, wo_ref, fw_ref, xo_ref, *maybe_y):
    z = z_ref[...].astype(F32)
    g = o_ref[...].astype(F32) * (z * _sigmoid(z))
    xn = x_ref[...] + jnp.dot(g.astype(BF16), wo_ref[...], preferred_element_type=F32)
    xo_ref[...] = xn
    if with_norm:
        maybe_y[0][...] = _rms(xn, fw_ref[...])


def _mla_out(o, z, x, w_out, final_w, with_norm):
    M, DI = o.shape
    D = x.shape[1]
    tm = min(M, ROW_TILE)
    row = lambda i: (i, 0)
    full = lambda i: (0, 0)
    n_out = 2 if with_norm else 1
    outs = pl.pallas_call(
        functools.partial(_mla_out_body, with_norm),
        grid=(M // tm,),
        in_specs=[pl.BlockSpec((tm, DI), row), pl.BlockSpec((tm, DI), row), pl.BlockSpec((tm, D), row),
                  pl.BlockSpec(w_out.shape, full), pl.BlockSpec((1, D), full)],
        out_specs=[pl.BlockSpec((tm, D), row)] * n_out,
        out_shape=[jax.ShapeDtypeStruct((M, D), F32)] * n_out,
        compiler_params=_params("arbitrary"),
        name="mla_out_proj",
    )(o, z, x, w_out, final_w)
    return outs if with_norm else (outs[0], None)


def _headwise_mm_body(x_ref, w_ref, o_ref):
    o_ref[0] = jnp.dot(x_ref[0].astype(BF16), w_ref[0], preferred_element_type=F32).astype(o_ref.dtype)


def _headwise_mm(x, w, out_dtype, name):
    H, B, K = x.shape
    N = w.shape[2]
    return pl.pallas_call(
        _headwise_mm_body,
        grid=(H,),
        in_specs=[pl.BlockSpec((1, B, K), lambda h: (h, 0, 0)), pl.BlockSpec((1, K, N), lambda h: (h, 0, 0))],
        out_specs=pl.BlockSpec((1, B, N), lambda h: (h, 0, 0)),
        out_shape=jax.ShapeDtypeStruct((H, B, N), out_dtype),
        compiler_params=_params("arbitrary"),
        name=name,
    )(x, w)


DECODE_PAGES_PER_STEP = 64


def _decode_update(ql_ref, qp_ref, cn_ref, pn_ref, ckv, kpe_t, o_ref, m_scr, l_scr, acc_scr):
    j = pl.program_id(1)
    ql = ql_ref[0]
    qp = qp_ref[0]

    @pl.when(j == 0)
    def _():
        cn = cn_ref[0]
        s_new = (jnp.sum(ql.astype(F32) * cn, axis=-1, keepdims=True)
                 + jnp.sum(qp.astype(F32) * pn_ref[0], axis=-1, keepdims=True))
        m_scr[...] = jnp.broadcast_to(s_new, m_scr.shape)
        l_scr[...] = jnp.ones(l_scr.shape, F32)
        acc_scr[...] = jnp.broadcast_to(cn, acc_scr.shape)

    s = (lax.dot_general(ql, ckv, _NT, preferred_element_type=F32)
         + jnp.dot(qp, kpe_t, preferred_element_type=F32))
    m_old = m_scr[...]
    m_new = jnp.maximum(m_old, jnp.max(s, axis=-1, keepdims=True))
    alpha = jnp.exp(m_old - m_new)
    p = jnp.exp(s - m_new[:, :1])
    l_new = alpha * l_scr[...] + jnp.sum(p, axis=-1, keepdims=True)
    acc = alpha[:, :1] * acc_scr[...] + jnp.dot(p.astype(BF16), ckv, preferred_element_type=F32)
    m_scr[...] = m_new
    l_scr[...] = l_new
    acc_scr[...] = acc

    @pl.when(j == pl.num_programs(1) - 1)
    def _():
        o_ref[0] = acc / l_new[:, :1]


def _decode_paged_body(n_pg, keep_packed, pt_ref, ql_ref, qp_ref, cn_ref, pn_ref, *refs):
    ckv_refs = refs[:n_pg]
    kpe_refs = refs[n_pg:2 * n_pg]
    rest = refs[2 * n_pg:]
    ckv = jnp.concatenate([r[0].astype(BF16) for r in ckv_refs], axis=0)
    kpe_t = jnp.concatenate([r[0].astype(BF16) for r in kpe_refs], axis=1)
    if keep_packed:
        o_ref, ckv_out, kpe_out = rest[:3]
        ckv_out[0] = ckv
        kpe_out[0] = kpe_t
        rest = (o_ref,) + rest[3:]
    _decode_update(ql_ref, qp_ref, cn_ref, pn_ref, ckv, kpe_t, *rest)


def _decode_packed_body(ql_ref, qp_ref, cn_ref, pn_ref, ckv_ref, kpe_ref, *rest):
    _decode_update(ql_ref, qp_ref, cn_ref, pn_ref, ckv_ref[0], kpe_ref[0], *rest)


def _decode_scratch(H, C):
    return [pltpu.VMEM((H, LANES), F32), pltpu.VMEM((H, LANES), F32), pltpu.VMEM((H, C), F32)]


def _decode_attn(q_lat, q_pe, ckv_new, kpe_new, cache_ckv, cache_kpe, page_table, keep_packed):
    B, H, C = q_lat.shape
    R = q_pe.shape[2]
    page = cache_ckv.shape[1]
    n_pages = page_table.shape[1]
    n_pg = min(DECODE_PAGES_PER_STEP, n_pages)
    ckv_spec = lambda i: pl.BlockSpec((1, page, C), lambda b, j, pt: (pt[b, j * n_pg + i], 0, 0))
    kpe_spec = lambda i: pl.BlockSpec((1, R, page), lambda b, j, pt: (pt[b, j * n_pg + i], 0, 0))
    per_b = lambda b, j, pt: (b, 0, 0)
    out_specs = [pl.BlockSpec((1, H, C), per_b)]
    out_shape = [jax.ShapeDtypeStruct((B, H, C), F32)]
    if keep_packed:
        out_specs += [pl.BlockSpec((1, n_pg * page, C), lambda b, j, pt: (b, j, 0)),
                      pl.BlockSpec((1, R, n_pg * page), lambda b, j, pt: (b, 0, j))]
        out_shape += [jax.ShapeDtypeStruct((B, n_pages * page, C), BF16),
                      jax.ShapeDtypeStruct((B, R, n_pages * page), BF16)]
    grid_spec = pltpu.PrefetchScalarGridSpec(
        num_scalar_prefetch=1,
        grid=(B, n_pages // n_pg),
        in_specs=[pl.BlockSpec((1, H, C), per_b), pl.BlockSpec((1, H, R), per_b),
                  pl.BlockSpec((1, 1, C), per_b), pl.BlockSpec((1, 1, R), per_b)]
                 + [ckv_spec(i) for i in range(n_pg)] + [kpe_spec(i) for i in range(n_pg)],
        out_specs=out_specs,
        scratch_shapes=_decode_scratch(H, C),
    )
    outs = pl.pallas_call(
        functools.partial(_decode_paged_body, n_pg, keep_packed),
        grid_spec=grid_spec,
        out_shape=out_shape,
        compiler_params=_params("arbitrary", "arbitrary"),
        name="mla_sample_attention",
    )(page_table, q_lat, q_pe, ckv_new, kpe_new, *([cache_ckv] * n_pg), *([cache_kpe] * n_pg))
    return outs if keep_packed else (outs[0], None, None)


def _decode_attn_packed(q_lat, q_pe, ckv_new, kpe_new, past_ckv, past_kpe_t, page):
    B, H, C = q_lat.shape
    R = q_pe.shape[2]
    past = past_ckv.shape[1]
    keys = min(past, DECODE_PAGES_PER_STEP * page)
    per_b = lambda b, j: (b, 0, 0)
    return pl.pallas_call(
        _decode_packed_body,
        grid=(B, past // keys),
        in_specs=[pl.BlockSpec((1, H, C), per_b), pl.BlockSpec((1, H, R), per_b),
                  pl.BlockSpec((1, 1, C), per_b), pl.BlockSpec((1, 1, R), per_b),
                  pl.BlockSpec((1, keys, C), lambda b, j: (b, j, 0)),
                  pl.BlockSpec((1, R, keys), lambda b, j: (b, 0, j))],
        out_specs=pl.BlockSpec((1, H, C), per_b),
        out_shape=jax.ShapeDtypeStruct((B, H, C), F32),
        scratch_shapes=_decode_scratch(H, C),
        compiler_params=_params("arbitrary", "arbitrary"),
        name="mla_sample_attention_packed",
    )(q_lat, q_pe, ckv_new, kpe_new, past_ckv, past_kpe_t)


def _rot_half_cols(w, r_dim):
    lead = w.shape[:-1]
    wb = w.reshape(lead + (-1, 2, r_dim // 2))
    return jnp.stack([-wb[..., 1, :], wb[..., 0, :]], axis=-2).reshape(w.shape)


def _rope_tables(pos, r_dim):
    half = r_dim // 2
    inv_freq = ROPE_THETA ** (-jnp.arange(half, dtype=F32) / half)
    ang = pos.astype(F32)[:, None] * inv_freq[None, :]
    cos, sin = jnp.cos(ang), jnp.sin(ang)
    return jnp.concatenate([cos, cos], axis=1), jnp.concatenate([sin, sin], axis=1)


def kernel(x_prompt, x_sample, state_wkv, state_shift, cache_ckv, cache_kpe, page_table, a_norm_w, a_mix, a_w_in, a_w0, a_w1, a_w2, a_a0, a_a1, a_a2, a_v0, a_v1, a_v2, a_k_k, a_k_a, a_r_k, a_lnx_w, a_lnx_b, a_w_out, kv_in_norm_w, w_dkv, kv_norm_w, w_uk, w_uv, b_norm_w, b_w_in, b_q_norm_w, b_w_uq, b_w_out, final_norm_w):
    Bp, Sp, D = x_prompt.shape
    Bs, Ts, _ = x_sample.shape
    assert Ts == 1
    n_a = a_norm_w.shape[0]
    n_b = b_norm_w.shape[0]
    DI = a_w_in.shape[3]
    H = DI // A_HEAD
    c_lat, n_bh, d_nope = w_uk.shape
    d_v = w_uv.shape[2]
    r_dim = cache_kpe.shape[2]
    past = page_table.shape[1] * cache_ckv.shape[1]
    scale = float(d_nope + r_dim) ** -0.5
    bf = lambda t: t.astype(BF16)
    vec = lambda t: t.reshape(1, -1)

    head_of_lane = jnp.arange(DI, dtype=jnp.int32) // A_HEAD
    ones_bd = (head_of_lane[:HEAD_SUM_WIDTH, None] == head_of_lane[None, :HEAD_SUM_WIDTH]).astype(BF16)

    xp = x_prompt.reshape(Bp * Sp, D)
    xs = x_sample.reshape(Bs, D)
    shift0 = jnp.zeros((Bp, D), F32)
    wkv0 = jnp.zeros((Bp, H, A_HEAD, A_HEAD), F32)
    vf_p = vf_s = None
    wkv_p, sh_p, sh_s = [], [], []
    state_t = jnp.transpose(state_wkv, (0, 2, 3, 4, 1))
    wkv_s_t = None

    for l in range(n_a):
        w_in = bf(a_w_in[l])
        lw_w = (bf(a_w1[l]), bf(a_w2[l]), vec(a_w0[l]))
        lora_a = (bf(a_a1[l]), bf(a_a2[l]), vec(a_a0[l]))
        lora_v = None if l == 0 else (bf(a_v1[l - 1]), bf(a_v2[l - 1]), vec(a_v0[l - 1]))
        k_k, k_a = vec(a_k_k[l]), vec(a_k_a[l])
        lnx_w, lnx_b, r_k = vec(a_lnx_w[l]), vec(a_lnx_b[l]), vec(a_r_k[l])
        w_out = bf(a_w_out[l])

        def layer(x, prev, s0, v_first, seq_len):
            mixed, hlast = _premix(x, prev, a_norm_w[l], a_mix[l], seq_len)
            rkvz = _bmm(mixed, w_in, BF16, 4)
            outs = _lora(mixed, rkvz, v_first, lw_w, lora_a, lora_v, k_k, ones_bd)
            lw, kkn, alr = outs[:3]
            v = (rkvz, 2) if lora_v is None else outs[3]
            if seq_len == 1:
                v2d = rkvz[2] if lora_v is None else v
                y, s_fin = _wkv_tok(l, state_t, s0, rkvz[0], lw, rkvz[1], v2d, kkn, alr, k_a)
            else:
                y, s_fin = _wkv_seq((rkvz, 0), (rkvz, 1), v, lw, kkn, alr, k_a, s0, seq_len)
            x_new = _rwkv_out(y, rkvz, v, alr, x, lnx_w, lnx_b, r_k, k_a, ones_bd, w_out)
            return x_new, (rkvz, 2), s_fin, hlast

        xp, v, S, last = layer(xp, shift0, wkv0, vf_p, Sp)
        wkv_p.append(S)
        sh_p.append(last)
        if l == 0:
            vf_p = v
        xs, v, wkv_s_t, last = layer(xs, state_shift[l], wkv_s_t, vf_s, 1)
        sh_s.append(last)
        if l == 0:
            vf_s = v

    pad = jnp.zeros((D, LANES - r_dim), F32)
    w_pe = w_dkv[:, c_lat:]
    w_dkv_ext = bf(jnp.concatenate([w_dkv[:, :c_lat], w_pe, pad, _rot_half_cols(w_pe, r_dim), pad], axis=1))
    cos_p, sin_p = _rope_tables(jnp.arange(Sp, dtype=jnp.int32), r_dim)
    cos_s, sin_s = _rope_tables(jnp.full((Bs,), past, dtype=jnp.int32), r_dim)
    ckv_p, kpe_p = _mla_kv(xp, vec(kv_in_norm_w), w_dkv_ext, vec(kv_norm_w), cos_p, sin_p, c_lat, r_dim, Sp)
    ckv_s, kpe_s = _mla_kv(xs, vec(kv_in_norm_w), w_dkv_ext, vec(kv_norm_w), cos_s, sin_s, c_lat, r_dim, 1)
    k_cat, v_heads_t = _mla_expand(ckv_p, kpe_p, bf(w_uk.reshape(c_lat, n_bh * d_nope)),
                                   bf(w_uv.reshape(c_lat, n_bh * d_v).T), n_bh, d_nope, d_v)
    w_uk_t = bf(jnp.transpose(w_uk, (1, 2, 0)))
    w_uv_h = bf(jnp.transpose(w_uv, (1, 0, 2)))
    cache_kpe_t = jnp.swapaxes(cache_kpe, 1, 2)
    tile_h = lambda t: jnp.tile(t, (1, n_bh))
    cos_pq, sin_pq, cos_sq, sin_sq = tile_h(cos_p), tile_h(sin_p), tile_h(cos_s), tile_h(sin_s)

    y_p = y_s = past_ckv = past_kpe_t = None
    for l in range(n_b):
        last_layer = l == n_b - 1
        w_in = bf(b_w_in[l])
        n0 = n_bh * d_nope
        uq = b_w_uq[l].reshape(-1, n_bh, d_nope + r_dim)
        uq_pe = uq[:, :, d_nope:].reshape(-1, n_bh * r_dim)
        w_uq_ext = bf(jnp.concatenate([uq[:, :, :d_nope].reshape(-1, n0), uq_pe, _rot_half_cols(uq_pe, r_dim)], axis=1))
        w_out = bf(b_w_out[l])
        nw, qnw, fw = vec(b_norm_w[l]), vec(b_q_norm_w[l]), vec(final_norm_w)

        q_cat, z = _mla_q(xp, nw, w_in, qnw, w_uq_ext, cos_pq, sin_pq, n_bh, d_nope, r_dim, scale, Sp)
        o = _flash(q_cat, k_cat, v_heads_t, Sp)
        xp, y_p = _mla_out(o, z, xp, w_out, fw, last_layer)

        q_cat, z = _mla_q(xs, nw, w_in, qnw, w_uq_ext, cos_sq, sin_sq, n_bh, d_nope, r_dim, scale, 1)
        q_lat = _headwise_mm(q_cat[:, :, :d_nope], w_uk_t, BF16, "mla_absorb_q")
        q_args = (jnp.transpose(q_lat, (1, 0, 2)), jnp.transpose(q_cat[:, :, d_nope:], (1, 0, 2)),
                  ckv_s.reshape(Bs, 1, c_lat), kpe_s.reshape(Bs, 1, r_dim))
        if past_ckv is None:
            o_lat, past_ckv, past_kpe_t = _decode_attn(*q_args, cache_ckv, cache_kpe_t, page_table, n_b > 1)
        else:
            o_lat = _decode_attn_packed(*q_args, past_ckv, past_kpe_t, cache_ckv.shape[1])
        o_h = _headwise_mm(jnp.transpose(o_lat, (1, 0, 2)), w_uv_h, F32, "mla_value_up")
        o = jnp.transpose(o_h, (1, 0, 2)).reshape(Bs, n_bh * d_v)
        xs, y_s = _mla_out(o, z, xs, w_out, fw, last_layer)

    return (y_p.reshape(Bp, Sp, D), y_s.reshape(Bs, Ts, D),
            jnp.stack(wkv_p), jnp.stack(sh_p), ckv_p.reshape(Bp, Sp, c_lat), kpe_p.reshape(Bp, Sp, r_dim),
            jnp.transpose(wkv_s_t, (0, 4, 1, 2, 3)), jnp.stack(sh_s),
            ckv_s.reshape(Bs, Ts, c_lat), kpe_s.reshape(Bs, Ts, r_dim))
```

```python
import functools
import math

import jax
import jax.numpy as jnp
from jax import lax
from jax.experimental import pallas as pl
from jax.experimental.pallas import tpu as pltpu

F32 = jnp.float32
BF16 = jnp.bfloat16

NORM_EPS = 1e-6
ROPE_THETA = 10000.0
A_HEAD = 64
GN_EPS = A_HEAD * 1e-5
WKV_CHUNK = 64
WKV_LANE_HEADS = 4
HEAD_SUM_WIDTH = 256
FLASH_BLOCK = 256
LANES = 128
ROW_TILE = 256
V7X_VMEM_LIMIT = 48 * 1024 * 1024

_NT = (((1,), (1,)), ((), ()))
_TN = (((0,), (0,)), ((), ()))


def _params(*sem):
    return pltpu.CompilerParams(dimension_semantics=sem, vmem_limit_bytes=V7X_VMEM_LIMIT)


def _dot(a, b):
    return jnp.dot(a.astype(BF16), b.astype(BF16), preferred_element_type=F32)


def _split2(x):
    hi = x.astype(BF16)
    lo = (x - hi.astype(F32)).astype(BF16)
    return hi, lo


def _head_sums(t, ones_bd, two_pass):
    w = ones_bd.shape[0]
    cols = []
    for i in range(t.shape[1] // w):
        blk = t[:, i * w:(i + 1) * w]
        if two_pass:
            hi, lo = _split2(blk)
            cols.append(jnp.dot(hi, ones_bd, preferred_element_type=F32)
                        + jnp.dot(lo, ones_bd, preferred_element_type=F32))
        else:
            cols.append(jnp.dot(blk.astype(BF16), ones_bd, preferred_element_type=F32))
    return jnp.concatenate(cols, axis=1)


def _sigmoid(x):
    return 1.0 / (1.0 + jnp.exp(-x))


def _rms(x, w):
    return x * lax.rsqrt(jnp.mean(x * x, axis=-1, keepdims=True) + NORM_EPS) * w


def _premix_seq_body(seq_tiles, x_ref, prev_ref, nw_ref, mix_ref, mixed_ref, hlast_ref, carry_ref):
    i = pl.program_id(0)
    h = _rms(x_ref[...], nw_ref[...])
    tm = h.shape[0]
    prev_row = jnp.where(i % seq_tiles == 0, prev_ref[0], carry_ref[...])
    row = lax.broadcasted_iota(jnp.int32, h.shape, 0)
    shifted = jnp.where(row == 0, prev_row, pltpu.roll(h, 1, axis=0))
    last = h[tm - 1:tm, :]
    carry_ref[...] = last
    hlast_ref[0] = last
    dx = shifted - h
    for p in range(6):
        mixed_ref[p] = (h + dx * mix_ref[p:p + 1, :]).astype(BF16)


def _premix_tok_body(x_ref, prev_ref, nw_ref, mix_ref, mixed_ref, hlast_ref):
    h = _rms(x_ref[...], nw_ref[...])
    hlast_ref[...] = h
    dx = prev_ref[...] - h
    for p in range(6):
        mixed_ref[p] = (h + dx * mix_ref[p:p + 1, :]).astype(BF16)


def _premix(x, prev, norm_w, mix, seq_len):
    M, D = x.shape
    B = M // seq_len
    nw = norm_w.reshape(1, D)
    out_mixed = jax.ShapeDtypeStruct((6, M, D), BF16)
    if seq_len == 1:
        tm = min(M, ROW_TILE)
        return pl.pallas_call(
            _premix_tok_body,
            grid=(M // tm,),
            in_specs=[pl.BlockSpec((tm, D), lambda i: (i, 0)),
                      pl.BlockSpec((tm, D), lambda i: (i, 0)),
                      pl.BlockSpec((1, D), lambda i: (0, 0)),
                      pl.BlockSpec((6, D), lambda i: (0, 0))],
            out_specs=[pl.BlockSpec((6, tm, D), lambda i: (0, i, 0)),
                       pl.BlockSpec((tm, D), lambda i: (i, 0))],
            out_shape=[out_mixed, jax.ShapeDtypeStruct((B, D), F32)],
            compiler_params=_params("arbitrary"),
            name="rwkv_premix_tok",
        )(x, prev, nw, mix)
    tm = min(seq_len, ROW_TILE)
    seq_tiles = seq_len // tm
    mixed, hlast = pl.pallas_call(
        functools.partial(_premix_seq_body, seq_tiles),
        grid=(M // tm,),
        in_specs=[pl.BlockSpec((tm, D), lambda i: (i, 0)),
                  pl.BlockSpec((1, 1, D), lambda i: (i // seq_tiles, 0, 0)),
                  pl.BlockSpec((1, D), lambda i: (0, 0)),
                  pl.BlockSpec((6, D), lambda i: (0, 0))],
        out_specs=[pl.BlockSpec((6, tm, D), lambda i: (0, i, 0)),
                   pl.BlockSpec((1, 1, D), lambda i: (i // seq_tiles, 0, 0))],
        out_shape=[out_mixed, jax.ShapeDtypeStruct((B, 1, D), F32)],
        scratch_shapes=[pltpu.VMEM((1, D), F32)],
        compiler_params=_params("arbitrary"),
        name="rwkv_premix_seq",
    )(x, prev.reshape(B, 1, D), nw, mix)
    return mixed, hlast.reshape(B, D)


def _bmm_body(x_ref, w_ref, o_ref):
    o_ref[0] = jnp.dot(x_ref[0], w_ref[0], preferred_element_type=F32).astype(o_ref.dtype)


def _bmm(x, w, out_dtype, n_batch, tm=512, tn=2048):
    _, M, K = x.shape
    P, _, N = w.shape
    assert P == n_batch
    tm, tn = min(tm, M), min(tn, N)
    return pl.pallas_call(
        _bmm_body,
        grid=(P, N // tn, M // tm),
        in_specs=[pl.BlockSpec((1, tm, K), lambda p, n, m: (p, m, 0)),
                  pl.BlockSpec((1, K, tn), lambda p, n, m: (p, 0, n))],
        out_specs=pl.BlockSpec((1, tm, tn), lambda p, n, m: (p, m, n)),
        out_shape=jax.ShapeDtypeStruct((P, M, N), out_dtype),
        compiler_params=_params("arbitrary", "arbitrary", "arbitrary"),
        name="rwkv_in_proj",
    )(x, w)


def _lora_body(has_vres, *refs):
    if has_vres:
        (xw_ref, xa_ref, xv_ref, k_ref, v_ref, vf_ref,
         w1_ref, w2_ref, w0_ref, a1_ref, a2_ref, a0_ref, v1_ref, v2_ref, v0_ref,
         kk_ref, ones_ref,
         lw_ref, kkn_ref, alr_ref, vp_ref) = refs
    else:
        (xw_ref, xa_ref, k_ref,
         w1_ref, w2_ref, w0_ref, a1_ref, a2_ref, a0_ref,
         kk_ref, ones_ref,
         lw_ref, kkn_ref, alr_ref) = refs
    xd = w0_ref[...] + _dot(jnp.tanh(_dot(xw_ref[0], w1_ref[...])), w2_ref[...])
    lw_ref[...] = -math.exp(-0.5) * _sigmoid(xd)
    alr_ref[...] = _sigmoid(a0_ref[...] + _dot(_dot(xa_ref[0], a1_ref[...]), a2_ref[...])).astype(BF16)
    kk = k_ref[0].astype(F32) * kk_ref[...]
    kkn_ref[...] = (kk / jnp.maximum(jnp.sqrt(_head_sums(kk * kk, ones_ref[...], True)), 1e-12)).astype(BF16)
    if has_vres:
        v = v_ref[0].astype(F32)
        vg = _sigmoid(v0_ref[...] + _dot(_dot(xv_ref[0], v1_ref[...]), v2_ref[...]))
        vp_ref[...] = (v + (vf_ref[0].astype(F32) - v) * vg).astype(BF16)


def _lora(mixed, rkvz, v_first, lw_w, lora_a, lora_v, k_k, ones_bd):
    _, M, D = mixed.shape
    DI = rkvz.shape[2]
    tm = min(M, ROW_TILE)
    has_vres = lora_v is not None
    row = lambda i: (i, 0)
    full = lambda i: (0, 0)
    mixed_spec = lambda p: pl.BlockSpec((1, tm, D), lambda i: (p, i, 0))
    rkvz_spec = lambda p: pl.BlockSpec((1, tm, DI), lambda i: (p, i, 0))
    wspec = lambda a: pl.BlockSpec(a.shape, full)
    ins = [mixed, mixed]
    specs = [mixed_spec(4), mixed_spec(5)]
    if has_vres:
        ins += [mixed]
        specs += [mixed_spec(2)]
    ins += [rkvz]
    specs += [rkvz_spec(1)]
    if has_vres:
        vf, vf_slab = _slab(v_first)
        ins += [rkvz, vf]
        specs += [rkvz_spec(2), pl.BlockSpec((1, tm, DI), lambda i: (vf_slab, i, 0))]
    weights = list(lw_w) + list(lora_a) + (list(lora_v) if has_vres else []) + [k_k, ones_bd]
    ins += weights
    specs += [wspec(a) for a in weights]
    out_dtypes = [F32, BF16, BF16] + ([BF16] if has_vres else [])
    return pl.pallas_call(
        functools.partial(_lora_body, has_vres),
        grid=(M // tm,),
        in_specs=specs,
        out_specs=[pl.BlockSpec((tm, DI), row)] * len(out_dtypes),
        out_shape=[jax.ShapeDtypeStruct((M, DI), dt) for dt in out_dtypes],
        compiler_params=_params("arbitrary"),
        name="rwkv_lora",
    )(*ins)


def _wkv_body(r_ref, k_ref, v_ref, lw_ref, kkn_ref, alr_ref, ka_ref, s0_ref, y_ref, sfin_ref, s_scr):
    j = pl.program_id(1)
    C = WKV_CHUNK
    N = A_HEAD
    GH = WKV_LANE_HEADS
    GL = GH * N
    assert C == N
    shift = N.bit_length() - 1
    groups = range(s_scr.shape[0])
    n_chunks = r_ref.shape[1] // C

    @pl.when(j == 0)
    def _():
        for g in groups:
            s_scr[g] = jnp.concatenate([s0_ref[0, g * GH + h] for h in range(GH)], axis=1)

    row = lax.broadcasted_iota(jnp.int32, (C, GL), 0)
    lane = lax.broadcasted_iota(jnp.int32, (C, GL), 1)
    lane_in_head = jnp.bitwise_and(lane, N - 1)
    lane_head = jnp.right_shift(lane, shift)
    strict = row > lane_in_head
    incl = row >= lane_in_head
    row2 = lax.broadcasted_iota(jnp.int32, (2 * C, GL), 0)
    lane2 = jnp.bitwise_and(lax.broadcasted_iota(jnp.int32, (2 * C, GL), 1), N - 1)
    strict_incl = row2 >= jnp.where(row2 < C, lane2 + 1, lane2 + C)
    eye = (row == lane_in_head).astype(F32)
    tri = (lax.broadcasted_iota(jnp.int32, (C, C), 0) >= lax.broadcasted_iota(jnp.int32, (C, C), 1)).astype(BF16)
    blockdiag = (jnp.right_shift(lax.broadcasted_iota(jnp.int32, (GL, GL), 0), shift)
                 == jnp.right_shift(lax.broadcasted_iota(jnp.int32, (GL, GL), 1), shift)).astype(BF16)

    def bd(x):
        return jnp.concatenate([x.astype(BF16)] * GH, axis=0) * blockdiag

    def dotf(a, b):
        return jnp.dot(a, b, preferred_element_type=F32)

    def chunk(c, carry):
        rows = pl.ds(pl.multiple_of(c * C, C), C)
        lw = lw_ref[0, rows, :]
        l_hi, l_lo = _split2(lw)
        cs = dotf(tri, l_hi) + dotf(tri, l_lo)
        cs_last = cs[C - 1:C, :]
        kkn, alr, k, r, v = (ref[0, rows, :].astype(F32) for ref in (kkn_ref, alr_ref, k_ref, r_ref, v_ref))
        a = -kkn
        b = kkn * alr
        k = k * (1.0 + (alr - 1.0) * ka_ref[...])
        e_neg = jnp.exp(-cs)
        e_rem = jnp.exp(cs_last - cs)
        at = (a * jnp.exp(cs - lw)).astype(BF16)
        rt = (r * jnp.exp(cs)).astype(BF16)
        bt = (b * e_neg).astype(BF16)
        kt = (k * e_neg).astype(BF16)
        bw = (b * e_rem).astype(BF16)
        kw = (k * e_rem).astype(BF16)
        vb = v.astype(BF16)
        w_end = jnp.exp(cs_last)
        gl = [slice(g * GL, (g + 1) * GL) for g in groups]
        ar = [jnp.concatenate([at[:, gl[g]], rt[:, gl[g]]], axis=0) for g in groups]
        bk = [jnp.concatenate([bd(bt[:, gl[g]]), bd(kt[:, gl[g]])], axis=0) for g in groups]
        p = [lax.dot_general(ar[g], bk[g], _NT, preferred_element_type=F32) for g in groups]
        low = [jnp.where(strict, p[g][:C, :GL], 0.0) for g in groups]
        m_k = [jnp.where(strict_incl, p[g][:, GL:], 0.0).astype(BF16) for g in groups]
        m_rb = [jnp.where(incl, p[g][C:, :GL], 0.0).astype(BF16) for g in groups]
        s_old = [s_scr[g] for g in groups]
        ay0 = [lax.dot_general(ar[g], bd(s_old[g]), _NT, preferred_element_type=F32) for g in groups]
        kv = [dotf(m_k[g], bd(vb[:, gl[g]])) for g in groups]
        r0 = [ay0[g][:C] + kv[g][:C] for g in groups]
        x = [dotf(low[g].astype(BF16), bd(low[g])) for g in groups]
        acc = [eye + low[g] for g in groups]
        span = 2
        while 2 * span < C:
            xa = [dotf(jnp.concatenate([x[g], acc[g]], axis=0).astype(BF16), bd(x[g])) for g in groups]
            x = [xa[g][:C] for g in groups]
            acc = [acc[g] + xa[g][C:] for g in groups]
            span *= 2
        inv = [acc[g] + dotf(acc[g].astype(BF16), bd(x[g])) for g in groups]
        ub = [dotf(inv[g].astype(BF16), bd(r0[g])).astype(BF16) for g in groups]
        for g in groups:
            y_ref[rows, gl[g]] = (ay0[g][C:] + kv[g][C:] + dotf(m_rb[g], bd(ub[g]))).astype(y_ref.dtype)
        uv = [jnp.concatenate([ub[g], vb[:, gl[g]]], axis=0) for g in groups]
        bkw = [jnp.concatenate([bw[:, gl[g]], kw[:, gl[g]]], axis=0) for g in groups]
        full = [lax.dot_general(uv[g], bkw[g], _TN, preferred_element_type=F32) for g in groups]
        for g in groups:
            upd = sum(jnp.where(lane_head == h, full[g][h * N:(h + 1) * N, :], 0.0) for h in range(GH))
            s_scr[g] = s_old[g] * w_end[:, gl[g]] + upd
        return carry

    lax.fori_loop(0, n_chunks, chunk, 0)

    @pl.when(j == pl.num_programs(1) - 1)
    def _():
        for g in groups:
            s = s_scr[g]
            for h in range(GH):
                sfin_ref[0, g * GH + h] = s[:, h * N:(h + 1) * N]


def _slab(t):
    return t if isinstance(t, tuple) else (t.reshape((1,) + t.shape), 0)


def _wkv_seq(r, k, v, lw, kkn, alr, k_a, s0, seq_len):
    srcs = [_slab(t) for t in (r, k, v, lw, kkn, alr)]
    _, M, DI = srcs[0][0].shape
    B = M // seq_len
    H = DI // A_HEAD
    tb = min(seq_len, ROW_TILE)
    nt = seq_len // tb
    blk = pl.BlockSpec((tb, DI), lambda bi, j: (bi * nt + j, 0))
    slab_blk = lambda p: pl.BlockSpec((1, tb, DI), lambda bi, j: (p, bi * nt + j, 0))
    sblk = pl.BlockSpec((1, H, A_HEAD, A_HEAD), lambda bi, j: (bi, 0, 0, 0))
    return pl.pallas_call(
        _wkv_body,
        grid=(B, nt),
        in_specs=[slab_blk(p) for _, p in srcs] + [pl.BlockSpec((1, DI), lambda bi, j: (0, 0)), sblk],
        out_specs=[blk, sblk],
        out_shape=[jax.ShapeDtypeStruct((M, DI), BF16), jax.ShapeDtypeStruct(s0.shape, F32)],
        scratch_shapes=[pltpu.VMEM((H // WKV_LANE_HEADS, A_HEAD, WKV_LANE_HEADS * A_HEAD), F32)],
        compiler_params=_params("arbitrary", "arbitrary"),
        name="rwkv_wkv_chunked",
    )(*(t for t, _ in srcs), k_a, s0)


WKV_TOK_HEADS = 2


def _wkv_tok_body(layer, s_ref, r_ref, lw_ref, k_ref, v_ref, kkn_ref, alr_ref, ka_ref, *rest):
    snew_ref, y_ref = rest[-2:]
    if snew_ref.shape[0] > 1:
        for other in range(snew_ref.shape[0]):
            if other != layer:
                snew_ref[other] = jnp.zeros(snew_ref.shape[1:], F32)
        snew_ref = snew_ref.at[layer:layer + 1]
    for i in range(s_ref.shape[1]):
        s = s_ref[0, i]
        kkn, alr = kkn_ref[i], alr_ref[i]
        k = k_ref[i] * (1.0 + (alr - 1.0) * ka_ref[i])
        sa = jnp.sum(s * (-kkn)[None], axis=1, keepdims=True)
        s_new = s * jnp.exp(lw_ref[i])[None] + sa * (kkn * alr)[None] + v_ref[i][:, None, :] * k[None]
        snew_ref[0, i] = s_new
        y_ref[i] = jnp.sum(s_new * r_ref[i][None], axis=1)


def _wkv_tok(layer, state_t, prev_out, r, lw, k, v, kkn, alr, k_a):
    L, H, N, _, B = state_t.shape
    hb = WKV_TOK_HEADS
    heads = lambda t: t.astype(F32).T.reshape(H, N, B)
    ka = jnp.broadcast_to(k_a.reshape(H, N, 1), (H, N, B))
    sspec = pl.BlockSpec((1, hb, N, N, B), lambda h: (layer, h, 0, 0, 0))
    vspec = pl.BlockSpec((hb, N, B), lambda h: (h, 0, 0))
    ins = [state_t] + [heads(t) for t in (r, lw, k, v, kkn, alr)] + [ka]
    specs = [sspec] + [vspec] * 7
    aliases = {}
    out_sspec = sspec
    if prev_out is None:
        out_sspec = pl.BlockSpec((L, hb, N, N, B), lambda h: (0, h, 0, 0, 0))
    else:
        aliases = {len(ins): 0}
        ins.append(prev_out)
        specs.append(pl.BlockSpec(memory_space=pl.ANY))
    s_out, y = pl.pallas_call(
        functools.partial(_wkv_tok_body, layer),
        grid=(H // hb,),
        in_specs=specs,
        out_specs=[out_sspec, vspec],
        out_shape=[jax.ShapeDtypeStruct(state_t.shape, F32), jax.ShapeDtypeStruct((H, N, B), F32)],
        input_output_aliases=aliases,
        compiler_params=_params("arbitrary"),
        name="rwkv_wkv_step",
    )(*ins)
    return y.reshape(H * N, B).T, s_out


def _rwkv_out_body(y_ref, r_ref, k_ref, v_ref, z_ref, alr_ref, x_ref, lw_ref, lb_ref, rk_ref, ka_ref,
                   ones_ref, wo_ref, o_ref):
    n = float(A_HEAD)
    ones_bd = ones_ref[...]
    y = y_ref[...].astype(F32)
    yc = y - _head_sums(y, ones_bd, True) / n
    var = _head_sums(yc * yc, ones_bd, False) / n
    yn = yc * lax.rsqrt(var + GN_EPS) * lw_ref[...] + lb_ref[...]
    r, k, v, z = (ref[0].astype(F32) for ref in (r_ref, k_ref, v_ref, z_ref))
    k = k * (1.0 + (alr_ref[...].astype(F32) - 1.0) * ka_ref[...])
    bonus = _head_sums(r * k * rk_ref[...], ones_bd, False) * v
    g = (yn + bonus) * (z * _sigmoid(z))
    o_ref[...] = x_ref[...] + jnp.dot(g.astype(BF16), wo_ref[...], preferred_element_type=F32)


def _rwkv_out(y, rkvz, v, alr, x, lnx_w, lnx_b, r_k, k_a, ones_bd, w_out):
    M, DI = y.shape
    D = x.shape[1]
    tm = min(M, ROW_TILE)
    row = lambda i: (i, 0)
    full = lambda i: (0, 0)
    slab_spec = lambda p: pl.BlockSpec((1, tm, DI), lambda i: (p, i, 0))
    v_in, v_slab = _slab(v)
    vec = pl.BlockSpec((1, DI), full)
    return pl.pallas_call(
        _rwkv_out_body,
        grid=(M // tm,),
        in_specs=[pl.BlockSpec((tm, DI), row), slab_spec(0), slab_spec(1), slab_spec(v_slab), slab_spec(3),
                  pl.BlockSpec((tm, DI), row), pl.BlockSpec((tm, D), row), vec, vec, vec, vec,
                  pl.BlockSpec(ones_bd.shape, full), pl.BlockSpec(w_out.shape, full)],
        out_specs=pl.BlockSpec((tm, D), row),
        out_shape=jax.ShapeDtypeStruct((M, D), F32),
        compiler_params=_params("arbitrary"),
        name="rwkv_out_proj",
    )(y, rkvz, rkvz, v_in, rkvz, alr, x, lnx_w, lnx_b, r_k, k_a, ones_bd, w_out)


def _mla_kv_body(c_lat, r_dim, x_ref, nw_ref, w_ref, kvn_ref, cos_ref, sin_ref, ckv_ref, kpe_ref):
    kv = _dot(_rms(x_ref[...], nw_ref[...]), w_ref[...])
    ckv_ref[...] = _rms(kv[:, :c_lat], kvn_ref[...])
    pe = kv[:, c_lat:c_lat + r_dim]
    pe_rot = kv[:, c_lat + LANES:c_lat + LANES + r_dim]
    kpe_ref[...] = pe * cos_ref[...] + pe_rot * sin_ref[...]


def _mla_kv(x, norm_w, w_ext, kv_norm_w, cos, sin, c_lat, r_dim, seq_len):
    M, D = x.shape
    tm = min(seq_len if seq_len > 1 else M, ROW_TILE)
    pos_tiles = max(seq_len // tm, 1)
    row = lambda i: (i, 0)
    full = lambda i: (0, 0)
    pos = lambda i: (i % pos_tiles, 0)
    return pl.pallas_call(
        functools.partial(_mla_kv_body, c_lat, r_dim),
        grid=(M // tm,),
        in_specs=[pl.BlockSpec((tm, D), row), pl.BlockSpec((1, D), full), pl.BlockSpec(w_ext.shape, full),
                  pl.BlockSpec((1, c_lat), full), pl.BlockSpec((tm, r_dim), pos), pl.BlockSpec((tm, r_dim), pos)],
        out_specs=[pl.BlockSpec((tm, c_lat), row), pl.BlockSpec((tm, r_dim), row)],
        out_shape=[jax.ShapeDtypeStruct((M, c_lat), F32), jax.ShapeDtypeStruct((M, r_dim), F32)],
        compiler_params=_params("arbitrary"),
        name="mla_shared_kv",
    )(x, norm_w, w_ext, kv_norm_w, cos, sin)


def _mla_expand_body(n_heads, d_nope, d_v, ckv_ref, kpe_ref, wk_ref, wvt_ref, k_ref, vt_ref):
    ckv = ckv_ref[...].astype(BF16)
    k_nope = jnp.dot(ckv, wk_ref[...], preferred_element_type=F32).astype(BF16)
    v_t = lax.dot_general(wvt_ref[...], ckv, _NT, preferred_element_type=F32).astype(BF16)
    pe = kpe_ref[...].astype(BF16)
    for h in range(n_heads):
        k_ref[h, :, :d_nope] = k_nope[:, h * d_nope:(h + 1) * d_nope]
        k_ref[h, :, d_nope:] = pe
        vt_ref[h] = v_t[h * d_v:(h + 1) * d_v, :]


def _mla_expand(ckv, kpe, w_uk, w_uv_t, n_heads, d_nope, d_v):
    M, c_lat = ckv.shape
    r_dim = kpe.shape[1]
    tm = min(M, ROW_TILE)
    row = lambda i: (i, 0)
    full = lambda i: (0, 0)
    return pl.pallas_call(
        functools.partial(_mla_expand_body, n_heads, d_nope, d_v),
        grid=(M // tm,),
        in_specs=[pl.BlockSpec((tm, c_lat), row), pl.BlockSpec((tm, r_dim), row),
                  pl.BlockSpec(w_uk.shape, full), pl.BlockSpec(w_uv_t.shape, full)],
        out_specs=[pl.BlockSpec((n_heads, tm, d_nope + r_dim), lambda i: (0, i, 0)),
                   pl.BlockSpec((n_heads, d_v, tm), lambda i: (0, 0, i))],
        out_shape=[jax.ShapeDtypeStruct((n_heads, M, d_nope + r_dim), BF16),
                   jax.ShapeDtypeStruct((n_heads, d_v, M), BF16)],
        compiler_params=_params("arbitrary"),
        name="mla_expand_kv",
    )(ckv, kpe, w_uk, w_uv_t)


def _mla_q_body(n_heads, d_nope, r_dim, q_lat, scale,
                x_ref, nw_ref, win_ref, qn_ref, wuq_ref, cos_ref, sin_ref, q_ref, z_ref):
    proj = _dot(_rms(x_ref[...], nw_ref[...]), win_ref[...])
    z_ref[...] = proj[:, q_lat:].astype(z_ref.dtype)
    q = _dot(_rms(proj[:, :q_lat], qn_ref[...]), wuq_ref[...]) * scale
    n0 = n_heads * d_nope
    n1 = n0 + n_heads * r_dim
    pe = q[:, n0:n1] * cos_ref[...] + q[:, n1:] * sin_ref[...]
    for h in range(n_heads):
        q_ref[h, :, :d_nope] = q[:, h * d_nope:(h + 1) * d_nope].astype(BF16)
        q_ref[h, :, d_nope:] = pe[:, h * r_dim:(h + 1) * r_dim].astype(BF16)


def _mla_q(x, norm_w, w_in, q_norm_w, w_uq_ext, cos_q, sin_q, n_heads, d_nope, r_dim, scale, seq_len):
    M, D = x.shape
    q_lat = q_norm_w.shape[1]
    d_gate = w_in.shape[1] - q_lat
    tm = min(seq_len if seq_len > 1 else M, ROW_TILE)
    pos_tiles = max(seq_len // tm, 1)
    row = lambda i: (i, 0)
    full = lambda i: (0, 0)
    pos = lambda i: (i % pos_tiles, 0)
    return pl.pallas_call(
        functools.partial(_mla_q_body, n_heads, d_nope, r_dim, q_lat, scale),
        grid=(M // tm,),
        in_specs=[pl.BlockSpec((tm, D), row), pl.BlockSpec((1, D), full), pl.BlockSpec(w_in.shape, full),
                  pl.BlockSpec((1, q_lat), full), pl.BlockSpec(w_uq_ext.shape, full),
                  pl.BlockSpec((tm, n_heads * r_dim), pos), pl.BlockSpec((tm, n_heads * r_dim), pos)],
        out_specs=[pl.BlockSpec((n_heads, tm, d_nope + r_dim), lambda i: (0, i, 0)),
                   pl.BlockSpec((tm, d_gate), row)],
        out_shape=[jax.ShapeDtypeStruct((n_heads, M, d_nope + r_dim), BF16),
                   jax.ShapeDtypeStruct((M, d_gate), BF16)],
        compiler_params=_params("arbitrary"),
        name="mla_query",
    )(x, norm_w, w_in, q_norm_w, w_uq_ext, cos_q, sin_q)


def _flash_body(t, q_ref, k_ref, vt_ref, o_ref):
    blocks = range(k_ref.shape[1] // t)
    causal = (lax.broadcasted_iota(jnp.int32, (t, t), 0) <= lax.broadcasted_iota(jnp.int32, (t, t), 1))
    blk = lambda ref, n: ref[0, n * t:(n + 1) * t, :]
    q = [blk(q_ref, n) for n in blocks]
    s_diag = [jnp.where(causal, lax.dot_general(blk(k_ref, n), q[n], _NT, preferred_element_type=F32), -jnp.inf)
              for n in blocks]
    s_past = [None] + [lax.dot_general(k_ref[0, :n * t, :], q[n], _NT, preferred_element_type=F32)
                       for n in blocks[1:]]
    m = [jnp.max(s_diag[n], axis=0, keepdims=True) for n in blocks]
    m = [m[0]] + [jnp.maximum(m[n], jnp.max(s_past[n], axis=0, keepdims=True)) for n in blocks[1:]]
    p_diag = [jnp.exp(s_diag[n] - m[n]) for n in blocks]
    p_past = [None] + [jnp.exp(s_past[n] - m[n]) for n in blocks[1:]]
    l = [jnp.sum(p_diag[n], axis=0, keepdims=True) for n in blocks]
    l = [l[0]] + [l[n] + jnp.sum(p_past[n], axis=0, keepdims=True) for n in blocks[1:]]
    o = [jnp.dot(vt_ref[0, :, n * t:(n + 1) * t], p_diag[n].astype(BF16), preferred_element_type=F32)
         for n in blocks]
    o = [o[0]] + [o[n] + jnp.dot(vt_ref[0, :, :n * t], p_past[n].astype(BF16), preferred_element_type=F32)
                  for n in blocks[1:]]
    for n in blocks:
        o_ref[n * t:(n + 1) * t, :] = (o[n] / l[n]).T.astype(o_ref.dtype)


def _flash(q, k, vt, seq_len):
    H, M, dk = q.shape
    dv = vt.shape[1]
    B = M // seq_len
    t = min(seq_len, FLASH_BLOCK)
    seq = lambda d: pl.BlockSpec((1, seq_len, d), lambda b, h: (h, b, 0))
    return pl.pallas_call(
        functools.partial(_flash_body, t),
        grid=(B, H),
        in_specs=[seq(dk), seq(dk), pl.BlockSpec((1, dv, seq_len), lambda b, h: (h, 0, b))],
        out_specs=pl.BlockSpec((seq_len, dv), lambda b, h: (b, h)),
        out_shape=jax.ShapeDtypeStruct((M, H * dv), BF16),
        compiler_params=_params("arbitrary", "arbitrary"),
        name="mla_prompt_attention",
    )(q, k, vt)


def _mla_out_body(with_norm, o_ref, z_ref, x_ref, wo_ref, fw_ref, xo_ref, *maybe_y):
    z = z_ref[...].astype(F32)
    g = o_ref[...].astype(F32) * (z * _sigmoid(z))
    xn = x_ref[...] + jnp.dot(g.astype(BF16), wo_ref[...], preferred_element_type=F32)
    xo_ref[...] = xn
    if with_norm:
        maybe_y[0][...] = _rms(xn, fw_ref[...])


def _mla_out(o, z, x, w_out, final_w, with_norm):
    M, DI = o.shape
    D = x.shape[1]
    tm = min(M, ROW_TILE)
    row = lambda i: (i, 0)
    full = lambda i: (0, 0)
    n_out = 2 if with_norm else 1
    outs = pl.pallas_call(
        functools.partial(_mla_out_body, with_norm),
        grid=(M // tm,),
        in_specs=[pl.BlockSpec((tm, DI), row), pl.BlockSpec((tm, DI), row), pl.BlockSpec((tm, D), row),
                  pl.BlockSpec(w_out.shape, full), pl.BlockSpec((1, D), full)],
        out_specs=[pl.BlockSpec((tm, D), row)] * n_out,
        out_shape=[jax.ShapeDtypeStruct((M, D), F32)] * n_out,
        compiler_params=_params("arbitrary"),
        name="mla_out_proj",
    )(o, z, x, w_out, final_w)
    return outs if with_norm else (outs[0], None)


def _headwise_mm_body(x_ref, w_ref, o_ref):
    o_ref[0] = jnp.dot(x_ref[0].astype(BF16), w_ref[0], preferred_element_type=F32).astype(o_ref.dtype)


def _headwise_mm(x, w, out_dtype, name):
    H, B, K = x.shape
    N = w.shape[2]
    return pl.pallas_call(
        _headwise_mm_body,
        grid=(H,),
        in_specs=[pl.BlockSpec((1, B, K), lambda h: (h, 0, 0)), pl.BlockSpec((1, K, N), lambda h: (h, 0, 0))],
        out_specs=pl.BlockSpec((1, B, N), lambda h: (h, 0, 0)),
        out_shape=jax.ShapeDtypeStruct((H, B, N), out_dtype),
        compiler_params=_params("arbitrary"),
        name=name,
    )(x, w)


DECODE_PAGES_PER_STEP = 64


def _decode_update(ql_ref, qp_ref, cn_ref, pn_ref, ckv, kpe_t, o_ref, m_scr, l_scr, acc_scr):
    j = pl.program_id(1)
    ql = ql_ref[0]
    qp = qp_ref[0]

    @pl.when(j == 0)
    def _():
        cn = cn_ref[0]
        s_new = (jnp.sum(ql.astype(F32) * cn, axis=-1, keepdims=True)
                 + jnp.sum(qp.astype(F32) * pn_ref[0], axis=-1, keepdims=True))
        m_scr[...] = jnp.broadcast_to(s_new, m_scr.shape)
        l_scr[...] = jnp.ones(l_scr.shape, F32)
        acc_scr[...] = jnp.broadcast_to(cn, acc_scr.shape)

    s = (lax.dot_general(ql, ckv, _NT, preferred_element_type=F32)
         + jnp.dot(qp, kpe_t, preferred_element_type=F32))
    m_old = m_scr[...]
    m_new = jnp.maximum(m_old, jnp.max(s, axis=-1, keepdims=True))
    alpha = jnp.exp(m_old - m_new)
    p = jnp.exp(s - m_new[:, :1])
    l_new = alpha * l_scr[...] + jnp.sum(p, axis=-1, keepdims=True)
    acc = alpha[:, :1] * acc_scr[...] + jnp.dot(p.astype(BF16), ckv, preferred_element_type=F32)
    m_scr[...] = m_new
    l_scr[...] = l_new
    acc_scr[...] = acc

    @pl.when(j == pl.num_programs(1) - 1)
    def _():
        o_ref[0] = acc / l_new[:, :1]


def _decode_paged_body(n_pg, keep_packed, pt_ref, ql_ref, qp_ref, cn_ref, pn_ref, *refs):
    ckv_refs = refs[:n_pg]
    kpe_refs = refs[n_pg:2 * n_pg]
    rest = refs[2 * n_pg:]
    ckv = jnp.concatenate([r[0].astype(BF16) for r in ckv_refs], axis=0)
    kpe_t = jnp.concatenate([r[0].astype(BF16) for r in kpe_refs], axis=1)
    if keep_packed:
        o_ref, ckv_out, kpe_out = rest[:3]
        ckv_out[0] = ckv
        kpe_out[0] = kpe_t
        rest = (o_ref,) + rest[3:]
    _decode_update(ql_ref, qp_ref, cn_ref, pn_ref, ckv, kpe_t, *rest)


def _decode_packed_body(ql_ref, qp_ref, cn_ref, pn_ref, ckv_ref, kpe_ref, *rest):
    _decode_update(ql_ref, qp_ref, cn_ref, pn_ref, ckv_ref[0], kpe_ref[0], *rest)


def _decode_scratch(H, C):
    return [pltpu.VMEM((H, LANES), F32), pltpu.VMEM((H, LANES), F32), pltpu.VMEM((H, C), F32)]


def _decode_attn(q_lat, q_pe, ckv_new, kpe_new, cache_ckv, cache_kpe, page_table, keep_packed):
    B, H, C = q_lat.shape
    R = q_pe.shape[2]
    page = cache_ckv.shape[1]
    n_pages = page_table.shape[1]
    n_pg = min(DECODE_PAGES_PER_STEP, n_pages)
    ckv_spec = lambda i: pl.BlockSpec((1, page, C), lambda b, j, pt: (pt[b, j * n_pg + i], 0, 0))
    kpe_spec = lambda i: pl.BlockSpec((1, R, page), lambda b, j, pt: (pt[b, j * n_pg + i], 0, 0))
    per_b = lambda b, j, pt: (b, 0, 0)
    out_specs = [pl.BlockSpec((1, H, C), per_b)]
    out_shape = [jax.ShapeDtypeStruct((B, H, C), F32)]
    if keep_packed:
        out_specs += [pl.BlockSpec((1, n_pg * page, C), lambda b, j, pt: (b, j, 0)),
                      pl.BlockSpec((1, R, n_pg * page), lambda b, j, pt: (b, 0, j))]
        out_shape += [jax.ShapeDtypeStruct((B, n_pages * page, C), BF16),
                      jax.ShapeDtypeStruct((B, R, n_pages * page), BF16)]
    grid_spec = pltpu.PrefetchScalarGridSpec(
        num_scalar_prefetch=1,
        grid=(B, n_pages // n_pg),
        in_specs=[pl.BlockSpec((1, H, C), per_b), pl.BlockSpec((1, H, R), per_b),
                  pl.BlockSpec((1, 1, C), per_b), pl.BlockSpec((1, 1, R), per_b)]
                 + [ckv_spec(i) for i in range(n_pg)] + [kpe_spec(i) for i in range(n_pg)],
        out_specs=out_specs,
        scratch_shapes=_decode_scratch(H, C),
    )
    outs = pl.pallas_call(
        functools.partial(_decode_paged_body, n_pg, keep_packed),
        grid_spec=grid_spec,
        out_shape=out_shape,
        compiler_params=_params("arbitrary", "arbitrary"),
        name="mla_sample_attention",
    )(page_table, q_lat, q_pe, ckv_new, kpe_new, *([cache_ckv] * n_pg), *([cache_kpe] * n_pg))
    return outs if keep_packed else (outs[0], None, None)


def _decode_attn_packed(q_lat, q_pe, ckv_new, kpe_new, past_ckv, past_kpe_t, page):
    B, H, C = q_lat.shape
    R = q_pe.shape[2]
    past = past_ckv.shape[1]
    keys = min(past, DECODE_PAGES_PER_STEP * page)
    per_b = lambda b, j: (b, 0, 0)
    return pl.pallas_call(
        _decode_packed_body,
        grid=(B, past // keys),
        in_specs=[pl.BlockSpec((1, H, C), per_b), pl.BlockSpec((1, H, R), per_b),
                  pl.BlockSpec((1, 1, C), per_b), pl.BlockSpec((1, 1, R), per_b),
                  pl.BlockSpec((1, keys, C), lambda b, j: (b, j, 0)),
                  pl.BlockSpec((1, R, keys), lambda b, j: (b, 0, j))],
        out_specs=pl.BlockSpec((1, H, C), per_b),
        out_shape=jax.ShapeDtypeStruct((B, H, C), F32),
        scratch_shapes=_decode_scratch(H, C),
        compiler_params=_params("arbitrary", "arbitrary"),
        name="mla_sample_attention_packed",
    )(q_lat, q_pe, ckv_new, kpe_new, past_ckv, past_kpe_t)


def _rot_half_cols(w, r_dim):
    lead = w.shape[:-1]
    wb = w.reshape(lead + (-1, 2, r_dim // 2))
    return jnp.stack([-wb[..., 1, :], wb[..., 0, :]], axis=-2).reshape(w.shape)


def _rope_tables(pos, r_dim):
    half = r_dim // 2
    inv_freq = ROPE_THETA ** (-jnp.arange(half, dtype=F32) / half)
    ang = pos.astype(F32)[:, None] * inv_freq[None, :]
    cos, sin = jnp.cos(ang), jnp.sin(ang)
    return jnp.concatenate([cos, cos], axis=1), jnp.concatenate([sin, sin], axis=1)


def kernel(x_prompt, x_sample, state_wkv, state_shift, cache_ckv, cache_kpe, page_table, a_norm_w, a_mix, a_w_in, a_w0, a_w1, a_w2, a_a0, a_a1, a_a2, a_v0, a_v1, a_v2, a_k_k, a_k_a, a_r_k, a_lnx_w, a_lnx_b, a_w_out, kv_in_norm_w, w_dkv, kv_norm_w, w_uk, w_uv, b_norm_w, b_w_in, b_q_norm_w, b_w_uq, b_w_out, final_norm_w):
    Bp, Sp, D = x_prompt.shape
    Bs, Ts, _ = x_sample.shape
    assert Ts == 1
    n_a = a_norm_w.shape[0]
    n_b = b_norm_w.shape[0]
    DI = a_w_in.shape[3]
    H = DI // A_HEAD
    c_lat, n_bh, d_nope = w_uk.shape
    d_v = w_uv.shape[2]
    r_dim = cache_kpe.shape[2]
    past = page_table.shape[1] * cache_ckv.shape[1]
    scale = float(d_nope + r_dim) ** -0.5
    bf = lambda t: t.astype(BF16)
    vec = lambda t: t.reshape(1, -1)

    head_of_lane = jnp.arange(DI, dtype=jnp.int32) // A_HEAD
    ones_bd = (head_of_lane[:HEAD_SUM_WIDTH, None] == head_of_lane[None, :HEAD_SUM_WIDTH]).astype(BF16)

    xp = x_prompt.reshape(Bp * Sp, D)
    xs = x_sample.reshape(Bs, D)
    shift0 = jnp.zeros((Bp, D), F32)
    wkv0 = jnp.zeros((Bp, H, A_HEAD, A_HEAD), F32)
    vf_p = vf_s = None
    wkv_p, sh_p, sh_s = [], [], []
    state_t = jnp.transpose(state_wkv, (0, 2, 3, 4, 1))
    wkv_s_t = None

    for l in range(n_a):
        w_in = bf(a_w_in[l])
        lw_w = (bf(a_w1[l]), bf(a_w2[l]), vec(a_w0[l]))
        lora_a = (bf(a_a1[l]), bf(a_a2[l]), vec(a_a0[l]))
        lora_v = None if l == 0 else (bf(a_v1[l - 1]), bf(a_v2[l - 1]), vec(a_v0[l - 1]))
        k_k, k_a = vec(a_k_k[l]), vec(a_k_a[l])
        lnx_w, lnx_b, r_k = vec(a_lnx_w[l]), vec(a_lnx_b[l]), vec(a_r_k[l])
        w_out = bf(a_w_out[l])

        def layer(x, prev, s0, v_first, seq_len):
            mixed, hlast = _premix(x, prev, a_norm_w[l], a_mix[l], seq_len)
            rkvz = _bmm(mixed, w_in, BF16, 4)
            outs = _lora(mixed, rkvz, v_first, lw_w, lora_a, lora_v, k_k, ones_bd)
            lw, kkn, alr = outs[:3]
            v = (rkvz, 2) if lora_v is None else outs[3]
            if seq_len == 1:
                v2d = rkvz[2] if lora_v is None else v
                y, s_fin = _wkv_tok(l, state_t, s0, rkvz[0], lw, rkvz[1], v2d, kkn, alr, k_a)
            else:
                y, s_fin = _wkv_seq((rkvz, 0), (rkvz, 1), v, lw, kkn, alr, k_a, s0, seq_len)
            x_new = _rwkv_out(y, rkvz, v, alr, x, lnx_w, lnx_b, r_k, k_a, ones_bd, w_out)
            return x_new, (rkvz, 2), s_fin, hlast

        xp, v, S, last = layer(xp, shift0, wkv0, vf_p, Sp)
        wkv_p.append(S)
        sh_p.append(last)
        if l == 0:
            vf_p = v
        xs, v, wkv_s_t, last = layer(xs, state_shift[l], wkv_s_t, vf_s, 1)
        sh_s.append(last)
        if l == 0:
            vf_s = v

    pad = jnp.zeros((D, LANES - r_dim), F32)
    w_pe = w_dkv[:, c_lat:]
    w_dkv_ext = bf(jnp.concatenate([w_dkv[:, :c_lat], w_pe, pad, _rot_half_cols(w_pe, r_dim), pad], axis=1))
    cos_p, sin_p = _rope_tables(jnp.arange(Sp, dtype=jnp.int32), r_dim)
    cos_s, sin_s = _rope_tables(jnp.full((Bs,), past, dtype=jnp.int32), r_dim)
    ckv_p, kpe_p = _mla_kv(xp, vec(kv_in_norm_w), w_dkv_ext, vec(kv_norm_w), cos_p, sin_p, c_lat, r_dim, Sp)
    ckv_s, kpe_s = _mla_kv(xs, vec(kv_in_norm_w), w_dkv_ext, vec(kv_norm_w), cos_s, sin_s, c_lat, r_dim, 1)
    k_cat, v_heads_t = _mla_expand(ckv_p, kpe_p, bf(w_uk.reshape(c_lat, n_bh * d_nope)),
                                   bf(w_uv.reshape(c_lat, n_bh * d_v).T), n_bh, d_nope, d_v)
    w_uk_t = bf(jnp.transpose(w_uk, (1, 2, 0)))
    w_uv_h = bf(jnp.transpose(w_uv, (1, 0, 2)))
    cache_kpe_t = jnp.swapaxes(cache_kpe, 1, 2)
    tile_h = lambda t: jnp.tile(t, (1, n_bh))
    cos_pq, sin_pq, cos_sq, sin_sq = tile_h(cos_p), tile_h(sin_p), tile_h(cos_s), tile_h(sin_s)

    y_p = y_s = past_ckv = past_kpe_t = None
    for l in range(n_b):
        last_layer = l == n_b - 1
        w_in = bf(b_w_in[l])
        n0 = n_bh * d_nope
        uq = b_w_uq[l].reshape(-1, n_bh, d_nope + r_dim)
        uq_pe = uq[:, :, d_nope:].reshape(-1, n_bh * r_dim)
        w_uq_ext = bf(jnp.concatenate([uq[:, :, :d_nope].reshape(-1, n0), uq_pe, _rot_half_cols(uq_pe, r_dim)], axis=1))
        w_out = bf(b_w_out[l])
        nw, qnw, fw = vec(b_norm_w[l]), vec(b_q_norm_w[l]), vec(final_norm_w)

        q_cat, z = _mla_q(xp, nw, w_in, qnw, w_uq_ext, cos_pq, sin_pq, n_bh, d_nope, r_dim, scale, Sp)
        o = _flash(q_cat, k_cat, v_heads_t, Sp)
        xp, y_p = _mla_out(o, z, xp, w_out, fw, last_layer)

        q_cat, z = _mla_q(xs, nw, w_in, qnw, w_uq_ext, cos_sq, sin_sq, n_bh, d_nope, r_dim, scale, 1)
        q_lat = _headwise_mm(q_cat[:, :, :d_nope], w_uk_t, BF16, "mla_absorb_q")
        q_args = (jnp.transpose(q_lat, (1, 0, 2)), jnp.transpose(q_cat[:, :, d_nope:], (1, 0, 2)),
                  ckv_s.reshape(Bs, 1, c_lat), kpe_s.reshape(Bs, 1, r_dim))
        if past_ckv is None:
            o_lat, past_ckv, past_kpe_t = _decode_attn(*q_args, cache_ckv, cache_kpe_t, page_table, n_b > 1)
        else:
            o_lat = _decode_attn_packed(*q_args, past_ckv, past_kpe_t, cache_ckv.shape[1])
        o_h = _headwise_mm(jnp.transpose(o_lat, (1, 0, 2)), w_uv_h, F32, "mla_value_up")
        o = jnp.transpose(o_h, (1, 0, 2)).reshape(Bs, n_bh * d_v)
        xs, y_s = _mla_out(o, z, xs, w_out, fw, last_layer)

    return (y_p.reshape(Bp, Sp, D), y_s.reshape(Bs, Ts, D),
            jnp.stack(wkv_p), jnp.stack(sh_p), ckv_p.reshape(Bp, Sp, c_lat), kpe_p.reshape(Bp, Sp, r_dim),
            jnp.transpose(wkv_s_t, (0, 4, 1, 2, 3)), jnp.stack(sh_s),
            ckv_s.reshape(Bs, Ts, c_lat), kpe_s.reshape(Bs, Ts, r_dim))
```

```python
import functools
import math

import jax
import jax.numpy as jnp
from jax import lax
from jax.experimental import pallas as pl
from jax.experimental.pallas import tpu as pltpu

F32 = jnp.float32
BF16 = jnp.bfloat16

NORM_EPS = 1e-6
ROPE_THETA = 10000.0
A_HEAD = 64
GN_EPS = A_HEAD * 1e-5
WKV_CHUNK = 64
WKV_LANE_HEADS = 4
HEAD_SUM_WIDTH = 256
FLASH_BLOCK = 256
LANES = 128
ROW_TILE = 512
V7X_VMEM_LIMIT = 48 * 1024 * 1024

_NT = (((1,), (1,)), ((), ()))
_TN = (((0,), (0,)), ((), ()))


def _params(*sem):
    return pltpu.CompilerParams(dimension_semantics=sem, vmem_limit_bytes=V7X_VMEM_LIMIT)


def _dot(a, b):
    return jnp.dot(a.astype(BF16), b.astype(BF16), preferred_element_type=F32)


def _split2(x):
    hi = x.astype(BF16)
    lo = (x - hi.astype(F32)).astype(BF16)
    return hi, lo


def _head_sums(t, ones_bd, two_pass):
    w = ones_bd.shape[0]
    cols = []
    for i in range(t.shape[1] // w):
        blk = t[:, i * w:(i + 1) * w]
        if two_pass:
            hi, lo = _split2(blk)
            cols.append(jnp.dot(hi, ones_bd, preferred_element_type=F32)
                        + jnp.dot(lo, ones_bd, preferred_element_type=F32))
        else:
            cols.append(jnp.dot(blk.astype(BF16), ones_bd, preferred_element_type=F32))
    return jnp.concatenate(cols, axis=1)


def _sigmoid(x):
    return 1.0 / (1.0 + jnp.exp(-x))


def _rms(x, w):
    return x * lax.rsqrt(jnp.mean(x * x, axis=-1, keepdims=True) + NORM_EPS) * w


def _premix_seq_body(seq_tiles, x_ref, prev_ref, nw_ref, mix_ref, mixed_ref, hlast_ref, carry_ref):
    i = pl.program_id(0)
    h = _rms(x_ref[...], nw_ref[...])
    tm = h.shape[0]
    prev_row = jnp.where(i % seq_tiles == 0, prev_ref[0], carry_ref[...])
    row = lax.broadcasted_iota(jnp.int32, h.shape, 0)
    shifted = jnp.where(row == 0, prev_row, pltpu.roll(h, 1, axis=0))
    last = h[tm - 1:tm, :]
    carry_ref[...] = last
    hlast_ref[0] = last
    dx = shifted - h
    for p in range(6):
        mixed_ref[p] = (h + dx * mix_ref[p:p + 1, :]).astype(BF16)


def _premix_tok_body(x_ref, prev_ref, nw_ref, mix_ref, mixed_ref, hlast_ref):
    h = _rms(x_ref[...], nw_ref[...])
    hlast_ref[...] = h
    dx = prev_ref[...] - h
    for p in range(6):
        mixed_ref[p] = (h + dx * mix_ref[p:p + 1, :]).astype(BF16)


def _premix(x, prev, norm_w, mix, seq_len):
    M, D = x.shape
    B = M // seq_len
    nw = norm_w.reshape(1, D)
    out_mixed = jax.ShapeDtypeStruct((6, M, D), BF16)
    if seq_len == 1:
        tm = min(M, ROW_TILE)
        return pl.pallas_call(
            _premix_tok_body,
            grid=(M // tm,),
            in_specs=[pl.BlockSpec((tm, D), lambda i: (i, 0)),
                      pl.BlockSpec((tm, D), lambda i: (i, 0)),
                      pl.BlockSpec((1, D), lambda i: (0, 0)),
                      pl.BlockSpec((6, D), lambda i: (0, 0))],
            out_specs=[pl.BlockSpec((6, tm, D), lambda i: (0, i, 0)),
                       pl.BlockSpec((tm, D), lambda i: (i, 0))],
            out_shape=[out_mixed, jax.ShapeDtypeStruct((B, D), F32)],
            compiler_params=_params("arbitrary"),
            name="rwkv_premix_tok",
        )(x, prev, nw, mix)
    tm = min(seq_len, ROW_TILE)
    seq_tiles = seq_len // tm
    mixed, hlast = pl.pallas_call(
        functools.partial(_premix_seq_body, seq_tiles),
        grid=(M // tm,),
        in_specs=[pl.BlockSpec((tm, D), lambda i: (i, 0)),
                  pl.BlockSpec((1, 1, D), lambda i: (i // seq_tiles, 0, 0)),
                  pl.BlockSpec((1, D), lambda i: (0, 0)),
                  pl.BlockSpec((6, D), lambda i: (0, 0))],
        out_specs=[pl.BlockSpec((6, tm, D), lambda i: (0, i, 0)),
                   pl.BlockSpec((1, 1, D), lambda i: (i // seq_tiles, 0, 0))],
        out_shape=[out_mixed, jax.ShapeDtypeStruct((B, 1, D), F32)],
        scratch_shapes=[pltpu.VMEM((1, D), F32)],
        compiler_params=_params("arbitrary"),
        name="rwkv_premix_seq",
    )(x, prev.reshape(B, 1, D), nw, mix)
    return mixed, hlast.reshape(B, D)


def _bmm_body(x_ref, w_ref, o_ref):
    o_ref[0] = jnp.dot(x_ref[0], w_ref[0], preferred_element_type=F32).astype(o_ref.dtype)


def _bmm(x, w, out_dtype, n_batch, tm=512, tn=2048):
    _, M, K = x.shape
    P, _, N = w.shape
    assert P == n_batch
    tm, tn = min(tm, M), min(tn, N)
    return pl.pallas_call(
        _bmm_body,
        grid=(P, N // tn, M // tm),
        in_specs=[pl.BlockSpec((1, tm, K), lambda p, n, m: (p, m, 0)),
                  pl.BlockSpec((1, K, tn), lambda p, n, m: (p, 0, n))],
        out_specs=pl.BlockSpec((1, tm, tn), lambda p, n, m: (p, m, n)),
        out_shape=jax.ShapeDtypeStruct((P, M, N), out_dtype),
        compiler_params=_params("arbitrary", "arbitrary", "arbitrary"),
        name="rwkv_in_proj",
    )(x, w)


def _lora_body(has_vres, *refs):
    if has_vres:
        (xw_ref, xa_ref, xv_ref, k_ref, v_ref, vf_ref,
         w1_ref, w2_ref, w0_ref, a1_ref, a2_ref, a0_ref, v1_ref, v2_ref, v0_ref,
         kk_ref, ones_ref,
         lw_ref, kkn_ref, alr_ref, vp_ref) = refs
    else:
        (xw_ref, xa_ref, k_ref,
         w1_ref, w2_ref, w0_ref, a1_ref, a2_ref, a0_ref,
         kk_ref, ones_ref,
         lw_ref, kkn_ref, alr_ref) = refs
    xd = w0_ref[...] + _dot(jnp.tanh(_dot(xw_ref[0], w1_ref[...])), w2_ref[...])
    lw_ref[...] = -math.exp(-0.5) * _sigmoid(xd)
    alr_ref[...] = _sigmoid(a0_ref[...] + _dot(_dot(xa_ref[0], a1_ref[...]), a2_ref[...])).astype(BF16)
    kk = k_ref[0].astype(F32) * kk_ref[...]
    kkn_ref[...] = (kk / jnp.maximum(jnp.sqrt(_head_sums(kk * kk, ones_ref[...], True)), 1e-12)).astype(BF16)
    if has_vres:
        v = v_ref[0].astype(F32)
        vg = _sigmoid(v0_ref[...] + _dot(_dot(xv_ref[0], v1_ref[...]), v2_ref[...]))
        vp_ref[...] = (v + (vf_ref[0].astype(F32) - v) * vg).astype(BF16)


def _lora(mixed, rkvz, v_first, lw_w, lora_a, lora_v, k_k, ones_bd):
    _, M, D = mixed.shape
    DI = rkvz.shape[2]
    tm = min(M, ROW_TILE)
    has_vres = lora_v is not None
    row = lambda i: (i, 0)
    full = lambda i: (0, 0)
    mixed_spec = lambda p: pl.BlockSpec((1, tm, D), lambda i: (p, i, 0))
    rkvz_spec = lambda p: pl.BlockSpec((1, tm, DI), lambda i: (p, i, 0))
    wspec = lambda a: pl.BlockSpec(a.shape, full)
    ins = [mixed, mixed]
    specs = [mixed_spec(4), mixed_spec(5)]
    if has_vres:
        ins += [mixed]
        specs += [mixed_spec(2)]
    ins += [rkvz]
    specs += [rkvz_spec(1)]
    if has_vres:
        vf, vf_slab = _slab(v_first)
        ins += [rkvz, vf]
        specs += [rkvz_spec(2), pl.BlockSpec((1, tm, DI), lambda i: (vf_slab, i, 0))]
    weights = list(lw_w) + list(lora_a) + (list(lora_v) if has_vres else []) + [k_k, ones_bd]
    ins += weights
    specs += [wspec(a) for a in weights]
    out_dtypes = [F32, BF16, BF16] + ([BF16] if has_vres else [])
    return pl.pallas_call(
        functools.partial(_lora_body, has_vres),
        grid=(M // tm,),
        in_specs=specs,
        out_specs=[pl.BlockSpec((tm, DI), row)] * len(out_dtypes),
        out_shape=[jax.ShapeDtypeStruct((M, DI), dt) for dt in out_dtypes],
        compiler_params=_params("arbitrary"),
        name="rwkv_lora",
    )(*ins)


def _wkv_body(r_ref, k_ref, v_ref, lw_ref, kkn_ref, alr_ref, ka_ref, s0_ref, y_ref, sfin_ref, s_scr):
    j = pl.program_id(1)
    C = WKV_CHUNK
    N = A_HEAD
    GH = WKV_LANE_HEADS
    GL = GH * N
    assert C == N
    shift = N.bit_length() - 1
    groups = range(s_scr.shape[0])
    n_chunks = r_ref.shape[1] // C

    @pl.when(j == 0)
    def _():
        for g in groups:
            s_scr[g] = jnp.concatenate([s0_ref[0, g * GH + h] for h in range(GH)], axis=1)

    row = lax.broadcasted_iota(jnp.int32, (C, GL), 0)
    lane = lax.broadcasted_iota(jnp.int32, (C, GL), 1)
    lane_in_head = jnp.bitwise_and(lane, N - 1)
    lane_head = jnp.right_shift(lane, shift)
    strict = row > lane_in_head
    incl = row >= lane_in_head
    row2 = lax.broadcasted_iota(jnp.int32, (2 * C, GL), 0)
    lane2 = jnp.bitwise_and(lax.broadcasted_iota(jnp.int32, (2 * C, GL), 1), N - 1)
    strict_incl = row2 >= jnp.where(row2 < C, lane2 + 1, lane2 + C)
    eye = (row == lane_in_head).astype(F32)
    tri = (lax.broadcasted_iota(jnp.int32, (C, C), 0) >= lax.broadcasted_iota(jnp.int32, (C, C), 1)).astype(BF16)
    blockdiag = (jnp.right_shift(lax.broadcasted_iota(jnp.int32, (GL, GL), 0), shift)
                 == jnp.right_shift(lax.broadcasted_iota(jnp.int32, (GL, GL), 1), shift)).astype(BF16)

    def bd(x):
        return jnp.concatenate([x.astype(BF16)] * GH, axis=0) * blockdiag

    def dotf(a, b):
        return jnp.dot(a, b, preferred_element_type=F32)

    def chunk(c, carry):
        rows = pl.ds(pl.multiple_of(c * C, C), C)
        lw = lw_ref[0, rows, :]
        l_hi, l_lo = _split2(lw)
        cs = dotf(tri, l_hi) + dotf(tri, l_lo)
        cs_last = cs[C - 1:C, :]
        kkn, alr, k, r, v = (ref[0, rows, :].astype(F32) for ref in (kkn_ref, alr_ref, k_ref, r_ref, v_ref))
        a = -kkn
        b = kkn * alr
        k = k * (1.0 + (alr - 1.0) * ka_ref[...])
        e_neg = jnp.exp(-cs)
        e_rem = jnp.exp(cs_last - cs)
        at = (a * jnp.exp(cs - lw)).astype(BF16)
        rt = (r * jnp.exp(cs)).astype(BF16)
        bt = (b * e_neg).astype(BF16)
        kt = (k * e_neg).astype(BF16)
        bw = (b * e_rem).astype(BF16)
        kw = (k * e_rem).astype(BF16)
        vb = v.astype(BF16)
        w_end = jnp.exp(cs_last)
        gl = [slice(g * GL, (g + 1) * GL) for g in groups]
        ar = [jnp.concatenate([at[:, gl[g]], rt[:, gl[g]]], axis=0) for g in groups]
        bk = [jnp.concatenate([bd(bt[:, gl[g]]), bd(kt[:, gl[g]])], axis=0) for g in groups]
        p = [lax.dot_general(ar[g], bk[g], _NT, preferred_element_type=F32) for g in groups]
        low = [jnp.where(strict, p[g][:C, :GL], 0.0) for g in groups]
        m_k = [jnp.where(strict_incl, p[g][:, GL:], 0.0).astype(BF16) for g in groups]
        m_rb = [jnp.where(incl, p[g][C:, :GL], 0.0).astype(BF16) for g in groups]
        s_old = [s_scr[g] for g in groups]
        ay0 = [lax.dot_general(ar[g], bd(s_old[g]), _NT, preferred_element_type=F32) for g in groups]
        kv = [dotf(m_k[g], bd(vb[:, gl[g]])) for g in groups]
        r0 = [ay0[g][:C] + kv[g][:C] for g in groups]
        x = [dotf(low[g].astype(BF16), bd(low[g])) for g in groups]
        acc = [eye + low[g] for g in groups]
        span = 2
        while 2 * span < C:
            xa = [dotf(jnp.concatenate([x[g], acc[g]], axis=0).astype(BF16), bd(x[g])) for g in groups]
            x = [xa[g][:C] for g in groups]
            acc = [acc[g] + xa[g][C:] for g in groups]
            span *= 2
        inv = [acc[g] + dotf(acc[g].astype(BF16), bd(x[g])) for g in groups]
        ub = [dotf(inv[g].astype(BF16), bd(r0[g])).astype(BF16) for g in groups]
        for g in groups:
            y_ref[rows, gl[g]] = (ay0[g][C:] + kv[g][C:] + dotf(m_rb[g], bd(ub[g]))).astype(y_ref.dtype)
        uv = [jnp.concatenate([ub[g], vb[:, gl[g]]], axis=0) for g in groups]
        bkw = [jnp.concatenate([bw[:, gl[g]], kw[:, gl[g]]], axis=0) for g in groups]
        full = [lax.dot_general(uv[g], bkw[g], _TN, preferred_element_type=F32) for g in groups]
        for g in groups:
            upd = sum(jnp.where(lane_head == h, full[g][h * N:(h + 1) * N, :], 0.0) for h in range(GH))
            s_scr[g] = s_old[g] * w_end[:, gl[g]] + upd
        return carry

    lax.fori_loop(0, n_chunks, chunk, 0)

    @pl.when(j == pl.num_programs(1) - 1)
    def _():
        for g in groups:
            s = s_scr[g]
            for h in range(GH):
                sfin_ref[0, g * GH + h] = s[:, h * N:(h + 1) * N]


def _slab(t):
    return t if isinstance(t, tuple) else (t.reshape((1,) + t.shape), 0)


def _wkv_seq(r, k, v, lw, kkn, alr, k_a, s0, seq_len):
    srcs = [_slab(t) for t in (r, k, v, lw, kkn, alr)]
    _, M, DI = srcs[0][0].shape
    B = M // seq_len
    H = DI // A_HEAD
    tb = min(seq_len, ROW_TILE)
    nt = seq_len // tb
    blk = pl.BlockSpec((tb, DI), lambda bi, j: (bi * nt + j, 0))
    slab_blk = lambda p: pl.BlockSpec((1, tb, DI), lambda bi, j: (p, bi * nt + j, 0))
    sblk = pl.BlockSpec((1, H, A_HEAD, A_HEAD), lambda bi, j: (bi, 0, 0, 0))
    return pl.pallas_call(
        _wkv_body,
        grid=(B, nt),
        in_specs=[slab_blk(p) for _, p in srcs] + [pl.BlockSpec((1, DI), lambda bi, j: (0, 0)), sblk],
        out_specs=[blk, sblk],
        out_shape=[jax.ShapeDtypeStruct((M, DI), BF16), jax.ShapeDtypeStruct(s0.shape, F32)],
        scratch_shapes=[pltpu.VMEM((H // WKV_LANE_HEADS, A_HEAD, WKV_LANE_HEADS * A_HEAD), F32)],
        compiler_params=_params("arbitrary", "arbitrary"),
        name="rwkv_wkv_chunked",
    )(*(t for t, _ in srcs), k_a, s0)


WKV_TOK_HEADS = 2


def _wkv_tok_body(layer, s_ref, r_ref, lw_ref, k_ref, v_ref, kkn_ref, alr_ref, ka_ref, *rest):
    snew_ref, y_ref = rest[-2:]
    if snew_ref.shape[0] > 1:
        for other in range(snew_ref.shape[0]):
            if other != layer:
                snew_ref[other] = jnp.zeros(snew_ref.shape[1:], F32)
        snew_ref = snew_ref.at[layer:layer + 1]
    for i in range(s_ref.shape[1]):
        s = s_ref[0, i]
        kkn, alr = kkn_ref[i], alr_ref[i]
        k = k_ref[i] * (1.0 + (alr - 1.0) * ka_ref[i])
        sa = jnp.sum(s * (-kkn)[None], axis=1, keepdims=True)
        s_new = s * jnp.exp(lw_ref[i])[None] + sa * (kkn * alr)[None] + v_ref[i][:, None, :] * k[None]
        snew_ref[0, i] = s_new
        y_ref[i] = jnp.sum(s_new * r_ref[i][None], axis=1)


def _wkv_tok(layer, state_t, prev_out, r, lw, k, v, kkn, alr, k_a):
    L, H, N, _, B = state_t.shape
    hb = WKV_TOK_HEADS
    heads = lambda t: t.astype(F32).T.reshape(H, N, B)
    ka = jnp.broadcast_to(k_a.reshape(H, N, 1), (H, N, B))
    sspec = pl.BlockSpec((1, hb, N, N, B), lambda h: (layer, h, 0, 0, 0))
    vspec = pl.BlockSpec((hb, N, B), lambda h: (h, 0, 0))
    ins = [state_t] + [heads(t) for t in (r, lw, k, v, kkn, alr)] + [ka]
    specs = [sspec] + [vspec] * 7
    aliases = {}
    out_sspec = sspec
    if prev_out is None:
        out_sspec = pl.BlockSpec((L, hb, N, N, B), lambda h: (0, h, 0, 0, 0))
    else:
        aliases = {len(ins): 0}
        ins.append(prev_out)
        specs.append(pl.BlockSpec(memory_space=pl.ANY))
    s_out, y = pl.pallas_call(
        functools.partial(_wkv_tok_body, layer),
        grid=(H // hb,),
        in_specs=specs,
        out_specs=[out_sspec, vspec],
        out_shape=[jax.ShapeDtypeStruct(state_t.shape, F32), jax.ShapeDtypeStruct((H, N, B), F32)],
        input_output_aliases=aliases,
        compiler_params=_params("arbitrary"),
        name="rwkv_wkv_step",
    )(*ins)
    return y.reshape(H * N, B).T, s_out


def _rwkv_out_body(y_ref, r_ref, k_ref, v_ref, z_ref, alr_ref, x_ref, lw_ref, lb_ref, rk_ref, ka_ref,
                   ones_ref, wo_ref, o_ref):
    n = float(A_HEAD)
    ones_bd = ones_ref[...]
    y = y_ref[...].astype(F32)
    yc = y - _head_sums(y, ones_bd, True) / n
    var = _head_sums(yc * yc, ones_bd, False) / n
    yn = yc * lax.rsqrt(var + GN_EPS) * lw_ref[...] + lb_ref[...]
    r, k, v, z = (ref[0].astype(F32) for ref in (r_ref, k_ref, v_ref, z_ref))
    k = k * (1.0 + (alr_ref[...].astype(F32) - 1.0) * ka_ref[...])
    bonus = _head_sums(r * k * rk_ref[...], ones_bd, False) * v
    g = (yn + bonus) * (z * _sigmoid(z))
    o_ref[...] = x_ref[...] + jnp.dot(g.astype(BF16), wo_ref[...], preferred_element_type=F32)


def _rwkv_out(y, rkvz, v, alr, x, lnx_w, lnx_b, r_k, k_a, ones_bd, w_out):
    M, DI = y.shape
    D = x.shape[1]
    tm = min(M, ROW_TILE)
    row = lambda i: (i, 0)
    full = lambda i: (0, 0)
    slab_spec = lambda p: pl.BlockSpec((1, tm, DI), lambda i: (p, i, 0))
    v_in, v_slab = _slab(v)
    vec = pl.BlockSpec((1, DI), full)
    return pl.pallas_call(
        _rwkv_out_body,
        grid=(M // tm,),
        in_specs=[pl.BlockSpec((tm, DI), row), slab_spec(0), slab_spec(1), slab_spec(v_slab), slab_spec(3),
                  pl.BlockSpec((tm, DI), row), pl.BlockSpec((tm, D), row), vec, vec, vec, vec,
                  pl.BlockSpec(ones_bd.shape, full), pl.BlockSpec(w_out.shape, full)],
        out_specs=pl.BlockSpec((tm, D), row),
        out_shape=jax.ShapeDtypeStruct((M, D), F32),
        compiler_params=_params("arbitrary"),
        name="rwkv_out_proj",
    )(y, rkvz, rkvz, v_in, rkvz, alr, x, lnx_w, lnx_b, r_k, k_a, ones_bd, w_out)


def _mla_kv_body(c_lat, r_dim, x_ref, nw_ref, w_ref, kvn_ref, cos_ref, sin_ref, ckv_ref, kpe_ref):
    kv = _dot(_rms(x_ref[...], nw_ref[...]), w_ref[...])
    ckv_ref[...] = _rms(kv[:, :c_lat], kvn_ref[...])
    pe = kv[:, c_lat:c_lat + r_dim]
    pe_rot = kv[:, c_lat + LANES:c_lat + LANES + r_dim]
    kpe_ref[...] = pe * cos_ref[...] + pe_rot * sin_ref[...]


def _mla_kv(x, norm_w, w_ext, kv_norm_w, cos, sin, c_lat, r_dim, seq_len):
    M, D = x.shape
    tm = min(seq_len if seq_len > 1 else M, ROW_TILE)
    pos_tiles = max(seq_len // tm, 1)
    row = lambda i: (i, 0)
    full = lambda i: (0, 0)
    pos = lambda i: (i % pos_tiles, 0)
    return pl.pallas_call(
        functools.partial(_mla_kv_body, c_lat, r_dim),
        grid=(M // tm,),
        in_specs=[pl.BlockSpec((tm, D), row), pl.BlockSpec((1, D), full), pl.BlockSpec(w_ext.shape, full),
                  pl.BlockSpec((1, c_lat), full), pl.BlockSpec((tm, r_dim), pos), pl.BlockSpec((tm, r_dim), pos)],
        out_specs=[pl.BlockSpec((tm, c_lat), row), pl.BlockSpec((tm, r_dim), row)],
        out_shape=[jax.ShapeDtypeStruct((M, c_lat), F32), jax.ShapeDtypeStruct((M, r_dim), F32)],
        compiler_params=_params("arbitrary"),
        name="mla_shared_kv",
    )(x, norm_w, w_ext, kv_norm_w, cos, sin)


def _mla_expand_body(n_heads, d_nope, d_v, ckv_ref, kpe_ref, wk_ref, wvt_ref, k_ref, vt_ref):
    ckv = ckv_ref[...].astype(BF16)
    k_nope = jnp.dot(ckv, wk_ref[...], preferred_element_type=F32).astype(BF16)
    v_t = lax.dot_general(wvt_ref[...], ckv, _NT, preferred_element_type=F32).astype(BF16)
    pe = kpe_ref[...].astype(BF16)
    for h in range(n_heads):
        k_ref[h, :, :d_nope] = k_nope[:, h * d_nope:(h + 1) * d_nope]
        k_ref[h, :, d_nope:] = pe
        vt_ref[h] = v_t[h * d_v:(h + 1) * d_v, :]


def _mla_expand(ckv, kpe, w_uk, w_uv_t, n_heads, d_nope, d_v):
    M, c_lat = ckv.shape
    r_dim = kpe.shape[1]
    tm = min(M, ROW_TILE)
    row = lambda i: (i, 0)
    full = lambda i: (0, 0)
    return pl.pallas_call(
        functools.partial(_mla_expand_body, n_heads, d_nope, d_v),
        grid=(M // tm,),
        in_specs=[pl.BlockSpec((tm, c_lat), row), pl.BlockSpec((tm, r_dim), row),
                  pl.BlockSpec(w_uk.shape, full), pl.BlockSpec(w_uv_t.shape, full)],
        out_specs=[pl.BlockSpec((n_heads, tm, d_nope + r_dim), lambda i: (0, i, 0)),
                   pl.BlockSpec((n_heads, d_v, tm), lambda i: (0, 0, i))],
        out_shape=[jax.ShapeDtypeStruct((n_heads, M, d_nope + r_dim), BF16),
                   jax.ShapeDtypeStruct((n_heads, d_v, M), BF16)],
        compiler_params=_params("arbitrary"),
        name="mla_expand_kv",
    )(ckv, kpe, w_uk, w_uv_t)


def _mla_q_body(n_heads, d_nope, r_dim, q_lat, scale,
                x_ref, nw_ref, win_ref, qn_ref, wuq_ref, cos_ref, sin_ref, q_ref, z_ref):
    proj = _dot(_rms(x_ref[...], nw_ref[...]), win_ref[...])
    z_ref[...] = proj[:, q_lat:].astype(z_ref.dtype)
    q = _dot(_rms(proj[:, :q_lat], qn_ref[...]), wuq_ref[...]) * scale
    n0 = n_heads * d_nope
    n1 = n0 + n_heads * r_dim
    pe = q[:, n0:n1] * cos_ref[...] + q[:, n1:] * sin_ref[...]
    for h in range(n_heads):
        q_ref[h, :, :d_nope] = q[:, h * d_nope:(h + 1) * d_nope].astype(BF16)
        q_ref[h, :, d_nope:] = pe[:, h * r_dim:(h + 1) * r_dim].astype(BF16)


def _mla_q(x, norm_w, w_in, q_norm_w, w_uq_ext, cos_q, sin_q, n_heads, d_nope, r_dim, scale, seq_len):
    M, D = x.shape
    q_lat = q_norm_w.shape[1]
    d_gate = w_in.shape[1] - q_lat
    tm = min(seq_len if seq_len > 1 else M, ROW_TILE)
    pos_tiles = max(seq_len // tm, 1)
    row = lambda i: (i, 0)
    full = lambda i: (0, 0)
    pos = lambda i: (i % pos_tiles, 0)
    return pl.pallas_call(
        functools.partial(_mla_q_body, n_heads, d_nope, r_dim, q_lat, scale),
        grid=(M // tm,),
        in_specs=[pl.BlockSpec((tm, D), row), pl.BlockSpec((1, D), full), pl.BlockSpec(w_in.shape, full),
                  pl.BlockSpec((1, q_lat), full), pl.BlockSpec(w_uq_ext.shape, full),
                  pl.BlockSpec((tm, n_heads * r_dim), pos), pl.BlockSpec((tm, n_heads * r_dim), pos)],
        out_specs=[pl.BlockSpec((n_heads, tm, d_nope + r_dim), lambda i: (0, i, 0)),
                   pl.BlockSpec((tm, d_gate), row)],
        out_shape=[jax.ShapeDtypeStruct((n_heads, M, d_nope + r_dim), BF16),
                   jax.ShapeDtypeStruct((M, d_gate), BF16)],
        compiler_params=_params("arbitrary"),
        name="mla_query",
    )(x, norm_w, w_in, q_norm_w, w_uq_ext, cos_q, sin_q)


def _flash_body(t, q_ref, k_ref, vt_ref, o_ref):
    blocks = range(k_ref.shape[1] // t)
    causal = (lax.broadcasted_iota(jnp.int32, (t, t), 0) <= lax.broadcasted_iota(jnp.int32, (t, t), 1))
    blk = lambda ref, n: ref[0, n * t:(n + 1) * t, :]
    q = [blk(q_ref, n) for n in blocks]
    s_diag = [jnp.where(causal, lax.dot_general(blk(k_ref, n), q[n], _NT, preferred_element_type=F32), -jnp.inf)
              for n in blocks]
    s_past = [None] + [lax.dot_general(k_ref[0, :n * t, :], q[n], _NT, preferred_element_type=F32)
                       for n in blocks[1:]]
    m = [jnp.max(s_diag[n], axis=0, keepdims=True) for n in blocks]
    m = [m[0]] + [jnp.maximum(m[n], jnp.max(s_past[n], axis=0, keepdims=True)) for n in blocks[1:]]
    p_diag = [jnp.exp(s_diag[n] - m[n]) for n in blocks]
    p_past = [None] + [jnp.exp(s_past[n] - m[n]) for n in blocks[1:]]
    l = [jnp.sum(p_diag[n], axis=0, keepdims=True) for n in blocks]
    l = [l[0]] + [l[n] + jnp.sum(p_past[n], axis=0, keepdims=True) for n in blocks[1:]]
    o = [jnp.dot(vt_ref[0, :, n * t:(n + 1) * t], p_diag[n].astype(BF16), preferred_element_type=F32)
         for n in blocks]
    o = [o[0]] + [o[n] + jnp.dot(vt_ref[0, :, :n * t], p_past[n].astype(BF16), preferred_element_type=F32)
                  for n in blocks[1:]]
    for n in blocks:
        o_ref[n * t:(n + 1) * t, :] = (o[n] / l[n]).T.astype(o_ref.dtype)


def _flash(q, k, vt, seq_len):
    H, M, dk = q.shape
    dv = vt.shape[1]
    B = M // seq_len
    t = min(seq_len, FLASH_BLOCK)
    seq = lambda d: pl.BlockSpec((1, seq_len, d), lambda b, h: (h, b, 0))
    return pl.pallas_call(
        functools.partial(_flash_body, t),
        grid=(B, H),
        in_specs=[seq(dk), seq(dk), pl.BlockSpec((1, dv, seq_len), lambda b, h: (h, 0, b))],
        out_specs=pl.BlockSpec((seq_len, dv), lambda b, h: (b, h)),
        out_shape=jax.ShapeDtypeStruct((M, H * dv), BF16),
        compiler_params=_params("arbitrary", "arbitrary"),
        name="mla_prompt_attention",
    )(q, k, vt)


def _mla_out_body(with_norm, o_ref, z_ref, x_ref, wo_ref, fw_ref, xo_ref, *maybe_y):
    z = z_ref[...].astype(F32)
    g = o_ref[...].astype(F32) * (z * _sigmoid(z))
    xn = x_ref[...] + jnp.dot(g.astype(BF16), wo_ref[...], preferred_element_type=F32)
    xo_ref[...] = xn
    if with_norm:
        maybe_y[0][...] = _rms(xn, fw_ref[...])


def _mla_out(o, z, x, w_out, final_w, with_norm):
    M, DI = o.shape
    D = x.shape[1]
    tm = min(M, ROW_TILE)
    row = lambda i: (i, 0)
    full = lambda i: (0, 0)
    n_out = 2 if with_norm else 1
    outs = pl.pallas_call(
        functools.partial(_mla_out_body, with_norm),
        grid=(M // tm,),
        in_specs=[pl.BlockSpec((tm, DI), row), pl.BlockSpec((tm, DI), row), pl.BlockSpec((tm, D), row),
                  pl.BlockSpec(w_out.shape, full), pl.BlockSpec((1, D), full)],
        out_specs=[pl.BlockSpec((tm, D), row)] * n_out,
        out_shape=[jax.ShapeDtypeStruct((M, D), F32)] * n_out,
        compiler_params=_params("arbitrary"),
        name="mla_out_proj",
    )(o, z, x, w_out, final_w)
    return outs if with_norm else (outs[0], None)


def _headwise_mm_body(x_ref, w_ref, o_ref):
    o_ref[0] = jnp.dot(x_ref[0].astype(BF16), w_ref[0], preferred_element_type=F32).astype(o_ref.dtype)


def _headwise_mm(x, w, out_dtype, name):
    H, B, K = x.shape
    N = w.shape[2]
    return pl.pallas_call(
        _headwise_mm_body,
        grid=(H,),
        in_specs=[pl.BlockSpec((1, B, K), lambda h: (h, 0, 0)), pl.BlockSpec((1, K, N), lambda h: (h, 0, 0))],
        out_specs=pl.BlockSpec((1, B, N), lambda h: (h, 0, 0)),
        out_shape=jax.ShapeDtypeStruct((H, B, N), out_dtype),
        compiler_params=_params("arbitrary"),
        name=name,
    )(x, w)


DECODE_PAGES_PER_STEP = 64


def _decode_update(ql_ref, qp_ref, cn_ref, pn_ref, ckv, kpe_t, o_ref, m_scr, l_scr, acc_scr):
    j = pl.program_id(1)
    ql = ql_ref[0]
    qp = qp_ref[0]

    @pl.when(j == 0)
    def _():
        cn = cn_ref[0]
        s_new = (jnp.sum(ql.astype(F32) * cn, axis=-1, keepdims=True)
                 + jnp.sum(qp.astype(F32) * pn_ref[0], axis=-1, keepdims=True))
        m_scr[...] = jnp.broadcast_to(s_new, m_scr.shape)
        l_scr[...] = jnp.ones(l_scr.shape, F32)
        acc_scr[...] = jnp.broadcast_to(cn, acc_scr.shape)

    s = (lax.dot_general(ql, ckv, _NT, preferred_element_type=F32)
         + jnp.dot(qp, kpe_t, preferred_element_type=F32))
    m_old = m_scr[...]
    m_new = jnp.maximum(m_old, jnp.max(s, axis=-1, keepdims=True))
    alpha = jnp.exp(m_old - m_new)
    p = jnp.exp(s - m_new[:, :1])
    l_new = alpha * l_scr[...] + jnp.sum(p, axis=-1, keepdims=True)
    acc = alpha[:, :1] * acc_scr[...] + jnp.dot(p.astype(BF16), ckv, preferred_element_type=F32)
    m_scr[...] = m_new
    l_scr[...] = l_new
    acc_scr[...] = acc

    @pl.when(j == pl.num_programs(1) - 1)
    def _():
        o_ref[0] = acc / l_new[:, :1]


def _decode_paged_body(n_pg, keep_packed, pt_ref, ql_ref, qp_ref, cn_ref, pn_ref, *refs):
    ckv_refs = refs[:n_pg]
    kpe_refs = refs[n_pg:2 * n_pg]
    rest = refs[2 * n_pg:]
    ckv = jnp.concatenate([r[0].astype(BF16) for r in ckv_refs], axis=0)
    kpe_t = jnp.concatenate([r[0].astype(BF16) for r in kpe_refs], axis=1)
    if keep_packed:
        o_ref, ckv_out, kpe_out = rest[:3]
        ckv_out[0] = ckv
        kpe_out[0] = kpe_t
        rest = (o_ref,) + rest[3:]
    _decode_update(ql_ref, qp_ref, cn_ref, pn_ref, ckv, kpe_t, *rest)


def _decode_packed_body(ql_ref, qp_ref, cn_ref, pn_ref, ckv_ref, kpe_ref, *rest):
    _decode_update(ql_ref, qp_ref, cn_ref, pn_ref, ckv_ref[0], kpe_ref[0], *rest)


def _decode_scratch(H, C):
    return [pltpu.VMEM((H, LANES), F32), pltpu.VMEM((H, LANES), F32), pltpu.VMEM((H, C), F32)]


def _decode_attn(q_lat, q_pe, ckv_new, kpe_new, cache_ckv, cache_kpe, page_table, keep_packed):
    B, H, C = q_lat.shape
    R = q_pe.shape[2]
    page = cache_ckv.shape[1]
    n_pages = page_table.shape[1]
    n_pg = min(DECODE_PAGES_PER_STEP, n_pages)
    ckv_spec = lambda i: pl.BlockSpec((1, page, C), lambda b, j, pt: (pt[b, j * n_pg + i], 0, 0))
    kpe_spec = lambda i: pl.BlockSpec((1, R, page), lambda b, j, pt: (pt[b, j * n_pg + i], 0, 0))
    per_b = lambda b, j, pt: (b, 0, 0)
    out_specs = [pl.BlockSpec((1, H, C), per_b)]
    out_shape = [jax.ShapeDtypeStruct((B, H, C), F32)]
    if keep_packed:
        out_specs += [pl.BlockSpec((1, n_pg * page, C), lambda b, j, pt: (b, j, 0)),
                      pl.BlockSpec((1, R, n_pg * page), lambda b, j, pt: (b, 0, j))]
        out_shape += [jax.ShapeDtypeStruct((B, n_pages * page, C), BF16),
                      jax.ShapeDtypeStruct((B, R, n_pages * page), BF16)]
    grid_spec = pltpu.PrefetchScalarGridSpec(
        num_scalar_prefetch=1,
        grid=(B, n_pages // n_pg),
        in_specs=[pl.BlockSpec((1, H, C), per_b), pl.BlockSpec((1, H, R), per_b),
                  pl.BlockSpec((1, 1, C), per_b), pl.BlockSpec((1, 1, R), per_b)]
                 + [ckv_spec(i) for i in range(n_pg)] + [kpe_spec(i) for i in range(n_pg)],
        out_specs=out_specs,
        scratch_shapes=_decode_scratch(H, C),
    )
    outs = pl.pallas_call(
        functools.partial(_decode_paged_body, n_pg, keep_packed),
        grid_spec=grid_spec,
        out_shape=out_shape,
        compiler_params=_params("arbitrary", "arbitrary"),
        name="mla_sample_attention",
    )(page_table, q_lat, q_pe, ckv_new, kpe_new, *([cache_ckv] * n_pg), *([cache_kpe] * n_pg))
    return outs if keep_packed else (outs[0], None, None)


def _decode_attn_packed(q_lat, q_pe, ckv_new, kpe_new, past_ckv, past_kpe_t, page):
    B, H, C = q_lat.shape
    R = q_pe.shape[2]
    past = past_ckv.shape[1]
    keys = min(past, DECODE_PAGES_PER_STEP * page)
    per_b = lambda b, j: (b, 0, 0)
    return pl.pallas_call(
        _decode_packed_body,
        grid=(B, past // keys),
        in_specs=[pl.BlockSpec((1, H, C), per_b), pl.BlockSpec((1, H, R), per_b),
                  pl.BlockSpec((1, 1, C), per_b), pl.BlockSpec((1, 1, R), per_b),
                  pl.BlockSpec((1, keys, C), lambda b, j: (b, j, 0)),
                  pl.BlockSpec((1, R, keys), lambda b, j: (b, 0, j))],
        out_specs=pl.BlockSpec((1, H, C), per_b),
        out_shape=jax.ShapeDtypeStruct((B, H, C), F32),
        scratch_shapes=_decode_scratch(H, C),
        compiler_params=_params("arbitrary", "arbitrary"),
        name="mla_sample_attention_packed",
    )(q_lat, q_pe, ckv_new, kpe_new, past_ckv, past_kpe_t)


def _rot_half_cols(w, r_dim):
    lead = w.shape[:-1]
    wb = w.reshape(lead + (-1, 2, r_dim // 2))
    return jnp.stack([-wb[..., 1, :], wb[..., 0, :]], axis=-2).reshape(w.shape)


def _rope_tables(pos, r_dim):
    half = r_dim // 2
    inv_freq = ROPE_THETA ** (-jnp.arange(half, dtype=F32) / half)
    ang = pos.astype(F32)[:, None] * inv_freq[None, :]
    cos, sin = jnp.cos(ang), jnp.sin(ang)
    return jnp.concatenate([cos, cos], axis=1), jnp.concatenate([sin, sin], axis=1)


def kernel(x_prompt, x_sample, state_wkv, state_shift, cache_ckv, cache_kpe, page_table, a_norm_w, a_mix, a_w_in, a_w0, a_w1, a_w2, a_a0, a_a1, a_a2, a_v0, a_v1, a_v2, a_k_k, a_k_a, a_r_k, a_lnx_w, a_lnx_b, a_w_out, kv_in_norm_w, w_dkv, kv_norm_w, w_uk, w_uv, b_norm_w, b_w_in, b_q_norm_w, b_w_uq, b_w_out, final_norm_w):
    Bp, Sp, D = x_prompt.shape
    Bs, Ts, _ = x_sample.shape
    assert Ts == 1
    n_a = a_norm_w.shape[0]
    n_b = b_norm_w.shape[0]
    DI = a_w_in.shape[3]
    H = DI // A_HEAD
    c_lat, n_bh, d_nope = w_uk.shape
    d_v = w_uv.shape[2]
    r_dim = cache_kpe.shape[2]
    past = page_table.shape[1] * cache_ckv.shape[1]
    scale = float(d_nope + r_dim) ** -0.5
    bf = lambda t: t.astype(BF16)
    vec = lambda t: t.reshape(1, -1)

    head_of_lane = jnp.arange(DI, dtype=jnp.int32) // A_HEAD
    ones_bd = (head_of_lane[:HEAD_SUM_WIDTH, None] == head_of_lane[None, :HEAD_SUM_WIDTH]).astype(BF16)

    xp = x_prompt.reshape(Bp * Sp, D)
    xs = x_sample.reshape(Bs, D)
    shift0 = jnp.zeros((Bp, D), F32)
    wkv0 = jnp.zeros((Bp, H, A_HEAD, A_HEAD), F32)
    vf_p = vf_s = None
    wkv_p, sh_p, sh_s = [], [], []
    state_t = jnp.transpose(state_wkv, (0, 2, 3, 4, 1))
    wkv_s_t = None

    for l in range(n_a):
        w_in = bf(a_w_in[l])
        lw_w = (bf(a_w1[l]), bf(a_w2[l]), vec(a_w0[l]))
        lora_a = (bf(a_a1[l]), bf(a_a2[l]), vec(a_a0[l]))
        lora_v = None if l == 0 else (bf(a_v1[l - 1]), bf(a_v2[l - 1]), vec(a_v0[l - 1]))
        k_k, k_a = vec(a_k_k[l]), vec(a_k_a[l])
        lnx_w, lnx_b, r_k = vec(a_lnx_w[l]), vec(a_lnx_b[l]), vec(a_r_k[l])
        w_out = bf(a_w_out[l])

        def layer(x, prev, s0, v_first, seq_len):
            mixed, hlast = _premix(x, prev, a_norm_w[l], a_mix[l], seq_len)
            rkvz = _bmm(mixed, w_in, BF16, 4)
            outs = _lora(mixed, rkvz, v_first, lw_w, lora_a, lora_v, k_k, ones_bd)
            lw, kkn, alr = outs[:3]
            v = (rkvz, 2) if lora_v is None else outs[3]
            if seq_len == 1:
                v2d = rkvz[2] if lora_v is None else v
                y, s_fin = _wkv_tok(l, state_t, s0, rkvz[0], lw, rkvz[1], v2d, kkn, alr, k_a)
            else:
                y, s_fin = _wkv_seq((rkvz, 0), (rkvz, 1), v, lw, kkn, alr, k_a, s0, seq_len)
            x_new = _rwkv_out(y, rkvz, v, alr, x, lnx_w, lnx_b, r_k, k_a, ones_bd, w_out)
            return x_new, (rkvz, 2), s_fin, hlast

        xp, v, S, last = layer(xp, shift0, wkv0, vf_p, Sp)
        wkv_p.append(S)
        sh_p.append(last)
        if l == 0:
            vf_p = v
        xs, v, wkv_s_t, last = layer(xs, state_shift[l], wkv_s_t, vf_s, 1)
        sh_s.append(last)
        if l == 0:
            vf_s = v

    pad = jnp.zeros((D, LANES - r_dim), F32)
    w_pe = w_dkv[:, c_lat:]
    w_dkv_ext = bf(jnp.concatenate([w_dkv[:, :c_lat], w_pe, pad, _rot_half_cols(w_pe, r_dim), pad], axis=1))
    cos_p, sin_p = _rope_tables(jnp.arange(Sp, dtype=jnp.int32), r_dim)
    cos_s, sin_s = _rope_tables(jnp.full((Bs,), past, dtype=jnp.int32), r_dim)
    ckv_p, kpe_p = _mla_kv(xp, vec(kv_in_norm_w), w_dkv_ext, vec(kv_norm_w), cos_p, sin_p, c_lat, r_dim, Sp)
    ckv_s, kpe_s = _mla_kv(xs, vec(kv_in_norm_w), w_dkv_ext, vec(kv_norm_w), cos_s, sin_s, c_lat, r_dim, 1)
    k_cat, v_heads_t = _mla_expand(ckv_p, kpe_p, bf(w_uk.reshape(c_lat, n_bh * d_nope)),
                                   bf(w_uv.reshape(c_lat, n_bh * d_v).T), n_bh, d_nope, d_v)
    w_uk_t = bf(jnp.transpose(w_uk, (1, 2, 0)))
    w_uv_h = bf(jnp.transpose(w_uv, (1, 0, 2)))
    cache_kpe_t = jnp.swapaxes(cache_kpe, 1, 2)
    tile_h = lambda t: jnp.tile(t, (1, n_bh))
    cos_pq, sin_pq, cos_sq, sin_sq = tile_h(cos_p), tile_h(sin_p), tile_h(cos_s), tile_h(sin_s)

    y_p = y_s = past_ckv = past_kpe_t = None
    for l in range(n_b):
        last_layer = l == n_b - 1
        w_in = bf(b_w_in[l])
        n0 = n_bh * d_nope
        uq = b_w_uq[l].reshape(-1, n_bh, d_nope + r_dim)
        uq_pe = uq[:, :, d_nope:].reshape(-1, n_bh * r_dim)
        w_uq_ext = bf(jnp.concatenate([uq[:, :, :d_nope].reshape(-1, n0), uq_pe, _rot_half_cols(uq_pe, r_dim)], axis=1))
        w_out = bf(b_w_out[l])
        nw, qnw, fw = vec(b_norm_w[l]), vec(b_q_norm_w[l]), vec(final_norm_w)

        q_cat, z = _mla_q(xp, nw, w_in, qnw, w_uq_ext, cos_pq, sin_pq, n_bh, d_nope, r_dim, scale, Sp)
        o = _flash(q_cat, k_cat, v_heads_t, Sp)
        xp, y_p = _mla_out(o, z, xp, w_out, fw, last_layer)

        q_cat, z = _mla_q(xs, nw, w_in, qnw, w_uq_ext, cos_sq, sin_sq, n_bh, d_nope, r_dim, scale, 1)
        q_lat = _headwise_mm(q_cat[:, :, :d_nope], w_uk_t, BF16, "mla_absorb_q")
        q_args = (jnp.transpose(q_lat, (1, 0, 2)), jnp.transpose(q_cat[:, :, d_nope:], (1, 0, 2)),
                  ckv_s.reshape(Bs, 1, c_lat), kpe_s.reshape(Bs, 1, r_dim))
        if past_ckv is None:
            o_lat, past_ckv, past_kpe_t = _decode_attn(*q_args, cache_ckv, cache_kpe_t, page_table, n_b > 1)
        else:
            o_lat = _decode_attn_packed(*q_args, past_ckv, past_kpe_t, cache_ckv.shape[1])
        o_h = _headwise_mm(jnp.transpose(o_lat, (1, 0, 2)), w_uv_h, F32, "mla_value_up")
        o = jnp.transpose(o_h, (1, 0, 2)).reshape(Bs, n_bh * d_v)
        xs, y_s = _mla_out(o, z, xs, w_out, fw, last_layer)

    return (y_p.reshape(Bp, Sp, D), y_s.reshape(Bs, Ts, D),
            jnp.stack(wkv_p), jnp.stack(sh_p), ckv_p.reshape(Bp, Sp, c_lat), kpe_p.reshape(Bp, Sp, r_dim),
            jnp.transpose(wkv_s_t, (0, 4, 1, 2, 3)), jnp.stack(sh_s),
            ckv_s.reshape(Bs, Ts, c_lat), kpe_s.reshape(Bs, Ts, r_dim))
```
